```python
import math
import jax, jax.numpy as jnp
from jax import lax
import numpy as np

D_MODEL = 1024
BATCH = 8
SEQ = 8192
DEPTH = 2
DEC_BATCH = 16
DEC_SEQ = 2048
PAST_LEN = 128

GRID_W = 64
Q_BLOCK = 128
HEAD_DIM = 64
EPS = 1e-6

HY_WIDTH = 512
HY_ORDER = 2
SHORT_K = 3
FILT_EMB = 33
FILT_ORDER = 64
FILT_INNER = 2
HY_FAST_DECAY = 0.3
HY_SLOW_DECAY = 1.5
HY_TARGET = 1e-2
HY_SHIFT = 0.0

GQA_HEADS = 8
GQA_KV_HEADS = 2
ROPE_THETA = 10000.0

DIFF_HEADS = 4
REL_BUCKETS = 32
REL_MAX_DIST = 128

N_BRANCH = 3
COL_SIZES = (
    (HY_ORDER + 1) * HY_WIDTH,
    HY_WIDTH,
    GQA_HEADS * HEAD_DIM,
    GQA_KV_HEADS * HEAD_DIM,
    GQA_KV_HEADS * HEAD_DIM,
    GQA_HEADS * HEAD_DIM,
    DIFF_HEADS * 2 * HEAD_DIM,
    DIFF_HEADS * 2 * HEAD_DIM,
    DIFF_HEADS * 2 * HEAD_DIM,
    DIFF_HEADS * 2 * HEAD_DIM,
)
IN_COLS = sum(COL_SIZES)
GQA_WIDTH = GQA_HEADS * HEAD_DIM
DIFF_WIDTH = DIFF_HEADS * 2 * HEAD_DIM

kernel_name = 'hybrid_hyena_gqa_diffattn_encoder'


def rms_norm(x, g):
    xf = x.astype(jnp.float32)
    y = xf * lax.rsqrt(jnp.mean(xf * xf, axis=-1, keepdims=True) + EPS)
    return (y * g.astype(jnp.float32)).astype(x.dtype)


def split_columns(u):
    parts, start = [], 0
    for size in COL_SIZES:
        parts.append(u[..., start:start + size])
        start += size
    return parts


def axial_rope_tables(L):
    rows = L // GRID_W
    row = jnp.broadcast_to(jnp.arange(rows, dtype=jnp.float32)[:, None], (rows, GRID_W)).reshape(L)
    col = jnp.broadcast_to(jnp.arange(GRID_W, dtype=jnp.float32)[None, :], (rows, GRID_W)).reshape(L)
    n_freq = HEAD_DIM // 4
    inv_freq = ROPE_THETA ** (-jnp.arange(n_freq, dtype=jnp.float32) / n_freq)
    ang = jnp.concatenate([row[:, None] * inv_freq, col[:, None] * inv_freq], axis=-1)
    return jnp.cos(ang), jnp.sin(ang)


def apply_rope(x, cos, sin):
    half = x.shape[-1] // 2
    x1, x2 = x[..., :half], x[..., half:]
    c, s = cos[None, :, None, :], sin[None, :, None, :]
    return jnp.concatenate([x1 * c - x2 * s, x2 * c + x1 * s], axis=-1).astype(x.dtype)


def t5_bucket(rel):
    nb = REL_BUCKETS // 2
    max_exact = nb // 2
    ret = (rel > 0).astype(jnp.int32) * nb
    n = jnp.abs(rel)
    nf = jnp.maximum(n, 1).astype(jnp.float32)
    large = max_exact + (jnp.log(nf / max_exact) / math.log(REL_MAX_DIST / max_exact)
                         * (nb - max_exact)).astype(jnp.int32)
    large = jnp.minimum(large, nb - 1)
    return ret + jnp.where(n < max_exact, n, large)


def hyena_positions(L):
    t01 = jnp.linspace(0.0, 1.0, L, dtype=jnp.float32)[:, None]
    bands = (FILT_EMB - 1) // 2
    w = 2.0 * math.pi * jnp.arange(L, dtype=jnp.float32)[:, None] / L
    f = jnp.linspace(1e-4, bands - 1, bands, dtype=jnp.float32)[None, :]
    z = jnp.concatenate([t01, jnp.cos(f * w), -jnp.sin(f * w)], axis=-1)
    max_decay = math.log(HY_TARGET) / HY_FAST_DECAY
    min_decay = math.log(HY_TARGET) / HY_SLOW_DECAY
    deltas = jnp.linspace(min_decay, max_decay, HY_WIDTH, dtype=jnp.float32)
    window = jnp.exp(-t01 * jnp.abs(deltas)[None, :]) + HY_SHIFT
    return z, window


def centred_short_conv(u, w, b):
    L = u.shape[1]
    pad = SHORT_K // 2
    up = jnp.pad(u, ((0, 0), (pad, pad), (0, 0)))
    out = up[:, 0:L] * w[0]
    for j in range(1, SHORT_K):
        out = out + up[:, j:j + L] * w[j]
    return out + b


def hyena_filter(z, window, w1, b1, w2, b2, wout, freq):
    a = jnp.sin(freq * (z @ w1 + b1))
    for i in range(FILT_INNER):
        a = jnp.sin(freq * (a @ w2[i] + b2[i]))
    hf = (a @ wout).reshape(-1, 2, HY_WIDTH) * window[:, None, :]
    fwd, bwd = hf[:, 0], hf[:, 1]
    return jnp.concatenate([fwd, jnp.zeros_like(fwd[:1]), bwd[:0:-1]], axis=0).astype(jnp.float32)


def bidir_fftconv(u, c):
    L = u.shape[1]
    uf = jnp.fft.rfft(u.astype(jnp.float32), n=2 * L, axis=1)
    cf = jnp.fft.rfft(c, n=2 * L, axis=0)
    y = jnp.fft.irfft(uf * cf[None], n=2 * L, axis=1)[:, :L]
    return y.astype(u.dtype)


def gqa_attention(q, k, v):
    B, L, H, D = q.shape
    G = H // GQA_KV_HEADS
    nb = L // Q_BLOCK
    qb = q.reshape(B, nb, Q_BLOCK, GQA_KV_HEADS, G, D).transpose(1, 0, 2, 3, 4, 5)
    scale = D ** -0.5

    def block(qi):
        s = jnp.einsum('bqkgd,bskd->bkgqs', qi, k).astype(jnp.float32) * scale
        p = jax.nn.softmax(s, axis=-1).astype(v.dtype)
        return jnp.einsum('bkgqs,bskd->bqkgd', p, v)

    o = lax.map(block, qb)
    return o.transpose(1, 0, 2, 3, 4, 5).reshape(B, L, H * D)


def diff_attention(q, k, v, rel_bias, lam, lam_init, subln_g):
    B, L, H, _, D = q.shape
    nb = L // Q_BLOCK
    qb = q.reshape(B, nb, Q_BLOCK, H, 2, D).transpose(1, 0, 2, 3, 4, 5)
    starts = jnp.arange(nb, dtype=jnp.int32) * Q_BLOCK
    kpos = jnp.arange(L, dtype=jnp.int32)
    scale = D ** -0.5

    def block(args):
        qi, q0 = args
        qpos = q0 + jnp.arange(Q_BLOCK, dtype=jnp.int32)
        bucket = t5_bucket(kpos[None, :] - qpos[:, None])
        bias = jnp.take(rel_bias, bucket, axis=0).transpose(2, 0, 1).astype(jnp.float32)
        s = jnp.einsum('bqhcd,bshcd->bhcqs', qi, k).astype(jnp.float32) * scale + bias[None, :, None]
        p = jax.nn.softmax(s, axis=-1)
        a = p[:, :, 0] - lam * p[:, :, 1]
        return jnp.einsum('bhqs,bshe->bqhe', a.astype(v.dtype), v)

    o = lax.map(block, (qb, starts))
    o = o.transpose(1, 0, 2, 3, 4).reshape(B, L, H, 2 * D)
    o = rms_norm(o, subln_g) * (1.0 - lam_init)
    return o.reshape(B, L, H * 2 * D)


def mixer_layer(x, l, p, rope_cos, rope_sin, z, window, rel_bias):
    B, L, _ = x.shape
    h = rms_norm(x, p['norm_g'][l])
    u = h @ p['w_in'][l]
    (hy_u, hy_gate, gq_q, gq_k, gq_v, gq_gate, df_q, df_k, df_v, df_gate) = split_columns(u)

    hy = centred_short_conv(hy_u, p['hy_conv_w'][l], p['hy_conv_b'][l])
    x0, x1, hv = hy[..., :HY_WIDTH], hy[..., HY_WIDTH:2 * HY_WIDTH], hy[..., 2 * HY_WIDTH:]
    filt = hyena_filter(z, window, p['hy_f_w1'][l], p['hy_f_b1'][l], p['hy_f_w2'][l],
                        p['hy_f_b2'][l], p['hy_f_wout'][l], p['hy_f_freq'][l])
    hv = hv * x1
    hv = bidir_fftconv(hv, filt) + hv * p['hy_bias'][l]
    y_hy = (hv * x0) * jax.nn.silu(hy_gate)

    q = gq_q.reshape(B, L, GQA_HEADS, HEAD_DIM)
    k = gq_k.reshape(B, L, GQA_KV_HEADS, HEAD_DIM)
    v = gq_v.reshape(B, L, GQA_KV_HEADS, HEAD_DIM)
    q = apply_rope(rms_norm(q, p['q_norm_g'][l]), rope_cos, rope_sin)
    k = apply_rope(rms_norm(k, p['k_norm_g'][l]), rope_cos, rope_sin)
    y_gq = gqa_attention(q, k, v) * jax.nn.silu(gq_gate)

    dq = df_q.reshape(B, L, DIFF_HEADS, 2, HEAD_DIM)
    dk = df_k.reshape(B, L, DIFF_HEADS, 2, HEAD_DIM)
    dv = df_v.reshape(B, L, DIFF_HEADS, 2 * HEAD_DIM)
    lam_init = 0.8 - 0.6 * math.exp(-0.3 * l)
    lam = (jnp.exp(jnp.sum(p['lam_q1'][l].astype(jnp.float32) * p['lam_k1'][l].astype(jnp.float32)))
           - jnp.exp(jnp.sum(p['lam_q2'][l].astype(jnp.float32) * p['lam_k2'][l].astype(jnp.float32)))
           + lam_init)
    y_df = diff_attention(dq, dk, dv, rel_bias, lam, lam_init, p['diff_subln_g'][l]) * jax.nn.silu(df_gate)

    gates = jax.nn.sigmoid((h @ p['w_merge'][l] + p['b_merge'][l]).astype(jnp.float32))
    gates = gates.reshape(B, L, N_BRANCH, D_MODEL).astype(x.dtype)
    merged = (gates[:, :, 0] * (y_hy @ p['w_branch_hy'][l])
              + gates[:, :, 1] * (y_gq @ p['w_branch_gqa'][l])
              + gates[:, :, 2] * (y_df @ p['w_branch_diff'][l]))
    return merged @ p['w_out'][l]


def encoder_trunk(x, p, rel_bias, final_g):
    L = x.shape[1]
    rope_cos, rope_sin = axial_rope_tables(L)
    z, window = hyena_positions(L)
    for l in range(DEPTH):
        x = x + mixer_layer(x, l, p, rope_cos, rope_sin, z, window, rel_bias)
    return rms_norm(x, final_g)


def setup_inputs(seed: int = 0) -> dict:
    key = jax.random.key(seed)
    ks = iter(list(jax.random.split(key, 40)))

    def nrm(shape, scale):
        return jax.random.normal(next(ks), shape, jnp.float32) * scale

    def gain(shape, base=1.0):
        return base + 0.02 * jax.random.normal(next(ks), shape, jnp.float32)

    d = {}
    d['x_prompt'] = nrm((BATCH, SEQ, D_MODEL), 1.0)
    d['x_sample'] = nrm((DEC_BATCH, DEC_SEQ, D_MODEL), 1.0)
    d['rel_bias'] = nrm((REL_BUCKETS, DIFF_HEADS), 0.5)
    d['norm_g'] = gain((DEPTH, D_MODEL))
    d['w_in'] = nrm((DEPTH, D_MODEL, IN_COLS), D_MODEL ** -0.5)
    d['hy_conv_w'] = nrm((DEPTH, SHORT_K, (HY_ORDER + 1) * HY_WIDTH), SHORT_K ** -0.5)
    d['hy_conv_b'] = nrm((DEPTH, (HY_ORDER + 1) * HY_WIDTH), 0.02)
    d['hy_f_w1'] = nrm((DEPTH, FILT_EMB, FILT_ORDER), FILT_EMB ** -0.5)
    d['hy_f_b1'] = nrm((DEPTH, FILT_ORDER), 0.1)
    d['hy_f_w2'] = nrm((DEPTH, FILT_INNER, FILT_ORDER, FILT_ORDER), FILT_ORDER ** -0.5)
    d['hy_f_b2'] = nrm((DEPTH, FILT_INNER, FILT_ORDER), 0.1)
    d['hy_f_wout'] = nrm((DEPTH, FILT_ORDER, 2 * HY_WIDTH), 0.05 * FILT_ORDER ** -0.5)
    d['hy_f_freq'] = gain((DEPTH, FILT_ORDER))
    d['hy_bias'] = nrm((DEPTH, HY_WIDTH), 1.0)
    d['q_norm_g'] = gain((DEPTH, HEAD_DIM))
    d['k_norm_g'] = gain((DEPTH, HEAD_DIM))
    d['lam_q1'] = nrm((DEPTH, HEAD_DIM), 0.1)
    d['lam_k1'] = nrm((DEPTH, HEAD_DIM), 0.1)
    d['lam_q2'] = nrm((DEPTH, HEAD_DIM), 0.1)
    d['lam_k2'] = nrm((DEPTH, HEAD_DIM), 0.1)
    d['diff_subln_g'] = gain((DEPTH, 2 * HEAD_DIM))
    d['w_branch_hy'] = nrm((DEPTH, HY_WIDTH, D_MODEL), HY_WIDTH ** -0.5)
    d['w_branch_gqa'] = nrm((DEPTH, GQA_WIDTH, D_MODEL), GQA_WIDTH ** -0.5)
    d['w_branch_diff'] = nrm((DEPTH, DIFF_WIDTH, D_MODEL), DIFF_WIDTH ** -0.5)
    d['w_merge'] = nrm((DEPTH, D_MODEL, N_BRANCH * D_MODEL), D_MODEL ** -0.5)
    d['b_merge'] = nrm((DEPTH, N_BRANCH * D_MODEL), 0.02)
    d['w_out'] = nrm((DEPTH, D_MODEL, D_MODEL), D_MODEL ** -0.5)
    d['final_g'] = gain((D_MODEL,))
    return d


def reference(x_prompt, x_sample, rel_bias, norm_g, w_in, hy_conv_w, hy_conv_b, hy_f_w1, hy_f_b1,
              hy_f_w2, hy_f_b2, hy_f_wout, hy_f_freq, hy_bias, q_norm_g, k_norm_g, lam_q1, lam_k1,
              lam_q2, lam_k2, diff_subln_g, w_branch_hy, w_branch_gqa, w_branch_diff, w_merge, b_merge,
              w_out, final_g):
    p = dict(norm_g=norm_g, w_in=w_in, hy_conv_w=hy_conv_w, hy_conv_b=hy_conv_b, hy_f_w1=hy_f_w1,
             hy_f_b1=hy_f_b1, hy_f_w2=hy_f_w2, hy_f_b2=hy_f_b2, hy_f_wout=hy_f_wout, hy_f_freq=hy_f_freq,
             hy_bias=hy_bias, q_norm_g=q_norm_g, k_norm_g=k_norm_g, lam_q1=lam_q1, lam_k1=lam_k1,
             lam_q2=lam_q2, lam_k2=lam_k2, diff_subln_g=diff_subln_g, w_branch_hy=w_branch_hy,
             w_branch_gqa=w_branch_gqa, w_branch_diff=w_branch_diff, w_merge=w_merge, b_merge=b_merge,
             w_out=w_out)
    y_prompt = encoder_trunk(x_prompt, p, rel_bias, final_g)
    y_sample = encoder_trunk(x_sample, p, rel_bias, final_g)
    return (y_prompt, y_sample)
```

```python
import functools
import math

import numpy as np
import jax
import jax.numpy as jnp
from jax import lax
from jax.experimental import pallas as pl
from jax.experimental.pallas import tpu as pltpu

D_MODEL = 1024
HEAD_DIM = 64
EPS = 1e-6
GRID_W = 64
ROPE_THETA = 10000.0

HY_WIDTH = 512
FILT_EMB = 33
FILT_EMB_PAD = 64
FILT_ORDER = 64
FILT_INNER = 2
HY_FAST_DECAY = 0.3
HY_SLOW_DECAY = 1.5
HY_TARGET = 1e-2
HY_SHIFT = 0.0

GQA_HEADS = 8
GQA_KV_HEADS = 2
GQA_GROUP = GQA_HEADS // GQA_KV_HEADS
DIFF_HEADS = 4
REL_BUCKETS = 32
REL_MAX_DIST = 128

C_HY_U = 0
C_HY_G = 1536
C_GQ_Q = 2048
C_GQ_K = 2560
C_GQ_V = 2688
C_GQ_G = 2816
C_DF_Q = 3328
C_DF_K = 3840
C_DF_V = 4352
C_DF_G = 4864
IN_COLS = 5376

LANES = 128
VMEM_LIMIT = 56 * 1024 * 1024
HI = lax.Precision.HIGHEST
F32 = jnp.float32
BF16 = jnp.bfloat16


def _cparams(sem):
    return pltpu.CompilerParams(dimension_semantics=sem, vmem_limit_bytes=VMEM_LIMIT)


def _silu(x):
    return x * (1.0 / (1.0 + jnp.exp(-x)))


def _dotf(a, b):
    return jnp.dot(a, b, preferred_element_type=F32)


def _norm_rope_t(xt, g_tab, cos, sin):
    ms = jnp.mean(xt * xt, axis=1, keepdims=True)
    xn = xt * lax.rsqrt(ms + EPS) * g_tab[None]
    half = HEAD_DIM // 2
    x1, x2 = xn[:, :half, :], xn[:, half:, :]
    c, s = cos[None], sin[None]
    return jnp.concatenate([x1 * c - x2 * s, x2 * c + x1 * s], axis=1)


def _in_proj_kernel(x_ref, ng_ref, w_ref, qg_ref, kg_ref, cos_ref, sin_ref,
                    hyu_ref, hysg_ref, gqq_ref, gqk_ref, gqv_ref, gqsg_ref,
                    dfq_ref, dfk_ref, dfv_ref, dfsg_ref):
    x = x_ref[0]
    tm = x.shape[0]
    ms = jnp.mean(x * x, axis=-1, keepdims=True)
    h = (x * lax.rsqrt(ms + EPS) * ng_ref[...]).astype(BF16)

    def proj(lo, hi):
        return _dotf(h, w_ref[:, lo:hi])

    hyu_ref[0] = proj(C_HY_U, C_HY_G)
    hysg_ref[0] = _silu(proj(C_HY_G, C_GQ_Q)).astype(BF16)

    cos, sin = cos_ref[...], sin_ref[...]
    scale = HEAD_DIM ** -0.5
    qt = proj(C_GQ_Q, C_GQ_K).T.reshape(GQA_HEADS, HEAD_DIM, tm)
    gqq_ref[0] = (_norm_rope_t(qt, qg_ref[...], cos, sin) * scale).astype(BF16)
    kt = proj(C_GQ_K, C_GQ_V).T.reshape(GQA_KV_HEADS, HEAD_DIM, tm)
    kt = _norm_rope_t(kt, kg_ref[...], cos, sin)
    for g in range(GQA_KV_HEADS):
        gqk_ref[0, g] = kt[g].T.astype(BF16)
    gqv_ref[0] = proj(C_GQ_V, C_GQ_G).T.reshape(GQA_KV_HEADS, HEAD_DIM, tm).astype(BF16)
    gqsg_ref[0] = _silu(proj(C_GQ_G, C_DF_Q)).astype(BF16)

    dfq_ref[0] = (proj(C_DF_Q, C_DF_K) * scale).T.reshape(DIFF_HEADS, 2 * HEAD_DIM, tm).astype(BF16)
    dk = proj(C_DF_K, C_DF_V).astype(BF16)
    for j in range(2 * DIFF_HEADS):
        dfk_ref[0, j] = dk[:, j * HEAD_DIM:(j + 1) * HEAD_DIM]
    dfv_ref[0] = proj(C_DF_V, C_DF_G).T.reshape(DIFF_HEADS, 2 * HEAD_DIM, tm).astype(BF16)
    dfsg_ref[0] = _silu(proj(C_DF_G, IN_COLS)).astype(BF16)


def _in_proj(x, norm_g, w_in_bf, q_g, k_g, cos_t, sin_t, tm):
    B, L, _ = x.shape
    nt = L // tm
    qg_tab = jnp.broadcast_to(q_g[:, None], (HEAD_DIM, tm))
    kg_tab = jnp.broadcast_to(k_g[:, None], (HEAD_DIM, tm))
    full = lambda shape: pl.BlockSpec(shape, lambda b, i: (0,) * len(shape))
    out_shapes = (
        jax.ShapeDtypeStruct((B, L, 3 * HY_WIDTH), F32),
        jax.ShapeDtypeStruct((B, L, HY_WIDTH), BF16),
        jax.ShapeDtypeStruct((B, GQA_HEADS, HEAD_DIM, L), BF16),
        jax.ShapeDtypeStruct((B, GQA_KV_HEADS, L, HEAD_DIM), BF16),
        jax.ShapeDtypeStruct((B, GQA_KV_HEADS, HEAD_DIM, L), BF16),
        jax.ShapeDtypeStruct((B, L, GQA_HEADS * HEAD_DIM), BF16),
        jax.ShapeDtypeStruct((B, DIFF_HEADS, 2 * HEAD_DIM, L), BF16),
        jax.ShapeDtypeStruct((B, 2 * DIFF_HEADS, L, HEAD_DIM), BF16),
        jax.ShapeDtypeStruct((B, DIFF_HEADS, 2 * HEAD_DIM, L), BF16),
        jax.ShapeDtypeStruct((B, L, DIFF_HEADS * 2 * HEAD_DIM), BF16),
    )
    tok = lambda w: pl.BlockSpec((1, tm, w), lambda b, i: (b, i, 0))
    tr = lambda h, d: pl.BlockSpec((1, h, d, tm), lambda b, i: (b, 0, 0, i))
    rows = lambda h: pl.BlockSpec((1, h, tm, HEAD_DIM), lambda b, i: (b, 0, i, 0))
    return pl.pallas_call(
        _in_proj_kernel,
        grid=(B, nt),
        in_specs=[
            tok(D_MODEL),
            full((1, D_MODEL)),
            full((D_MODEL, IN_COLS)),
            full((HEAD_DIM, tm)),
            full((HEAD_DIM, tm)),
            pl.BlockSpec((HEAD_DIM // 2, tm), lambda b, i: (0, i)),
            pl.BlockSpec((HEAD_DIM // 2, tm), lambda b, i: (0, i)),
        ],
        out_specs=(
            tok(3 * HY_WIDTH), tok(HY_WIDTH),
            tr(GQA_HEADS, HEAD_DIM), rows(GQA_KV_HEADS), tr(GQA_KV_HEADS, HEAD_DIM), tok(GQA_HEADS * HEAD_DIM),
            tr(DIFF_HEADS, 2 * HEAD_DIM), rows(2 * DIFF_HEADS), tr(DIFF_HEADS, 2 * HEAD_DIM),
            tok(DIFF_HEADS * 2 * HEAD_DIM),
        ),
        out_shape=out_shapes,
        compiler_params=_cparams(("parallel", "parallel")),
        name="in_proj",
    )(x, norm_g.reshape(1, D_MODEL), w_in_bf, qg_tab, kg_tab, cos_t, sin_t)


def _hy_pre_kernel(u_ref, prev_ref, next_ref, w_ref, b_ref, sg_ref, hv_ref, x0g_ref):
    i = pl.program_id(1)
    nt = pl.num_programs(1)
    u = u_ref[0]
    tl = u.shape[0]
    prev_row = jnp.where(i > 0, prev_ref[0, 0, 7:8, :], 0.0)
    next_row = jnp.where(i < nt - 1, next_ref[0, 0, 0:1, :], 0.0)
    row = lax.broadcasted_iota(jnp.int32, u.shape, 0)
    up = jnp.where(row == 0, prev_row, pltpu.roll(u, 1, 0))
    dn = jnp.where(row == tl - 1, next_row, pltpu.roll(u, tl - 1, 0))
    w = w_ref[...]
    hy = up * w[0:1] + u * w[1:2] + dn * w[2:3] + b_ref[...]
    x0 = hy[:, :HY_WIDTH]
    x1 = hy[:, HY_WIDTH:2 * HY_WIDTH]
    hv = hy[:, 2 * HY_WIDTH:]
    hv_ref[0] = hv * x1
    x0g_ref[0] = (x0 * sg_ref[0].astype(F32)).astype(x0g_ref.dtype)


def _hy_pre(hy_u, conv_w, conv_b, hy_sg, tl):
    B, L, W3 = hy_u.shape
    nt = L // tl
    g8 = tl // 8
    u4 = hy_u.reshape(B, L // 8, 8, W3)
    return pl.pallas_call(
        _hy_pre_kernel,
        grid=(B, nt),
        in_specs=[
            pl.BlockSpec((1, tl, W3), lambda b, i: (b, i, 0)),
            pl.BlockSpec((1, 1, 8, W3), lambda b, i: (b, jnp.maximum(i * g8 - 1, 0), 0, 0)),
            pl.BlockSpec((1, 1, 8, W3), lambda b, i: (b, jnp.minimum((i + 1) * g8, L // 8 - 1), 0, 0)),
            pl.BlockSpec((3, W3), lambda b, i: (0, 0)),
            pl.BlockSpec((1, W3), lambda b, i: (0, 0)),
            pl.BlockSpec((1, tl, HY_WIDTH), lambda b, i: (b, i, 0)),
        ],
        out_specs=(
            pl.BlockSpec((1, tl, HY_WIDTH), lambda b, i: (b, i, 0)),
            pl.BlockSpec((1, tl, HY_WIDTH), lambda b, i: (b, i, 0)),
        ),
        out_shape=(
            jax.ShapeDtypeStruct((B, L, HY_WIDTH), F32),
            jax.ShapeDtypeStruct((B, L, HY_WIDTH), BF16),
        ),
        compiler_params=_cparams(("parallel", "parallel")),
        name="hy_pre",
    )(hy_u, u4, u4, conv_w, conv_b.reshape(1, W3), hy_sg)


def _hy_filter_kernel(z_ref, win_ref, w1_ref, b1_ref, w2_ref, b2_ref, wout_ref, freq_ref, kf_ref, kb_ref):
    i = pl.program_id(0)
    freq = freq_ref[...]
    a = jnp.sin(freq * (jnp.dot(z_ref[...], w1_ref[...], precision=HI, preferred_element_type=F32) + b1_ref[...]))
    for j in range(FILT_INNER):
        a = jnp.sin(freq * (jnp.dot(a, w2_ref[j], precision=HI, preferred_element_type=F32) + b2_ref[j:j + 1, :]))
    hf = jnp.dot(a, wout_ref[...], precision=HI, preferred_element_type=F32)
    win = win_ref[...]
    kf_ref[...] = hf[:, :HY_WIDTH] * win
    row = lax.broadcasted_iota(jnp.int32, win.shape, 0)
    kb_ref[...] = jnp.where((row == 0) & (i == 0), 0.0, hf[:, HY_WIDTH:] * win)


def _hy_filter(z_pad, window, w1_pad, b1, w2, b2, wout, freq, tl):
    L = z_pad.shape[0]
    full = lambda shape: pl.BlockSpec(shape, lambda i: (0,) * len(shape))
    return pl.pallas_call(
        _hy_filter_kernel,
        grid=(L // tl,),
        in_specs=[
            pl.BlockSpec((tl, FILT_EMB_PAD), lambda i: (i, 0)),
            pl.BlockSpec((tl, HY_WIDTH), lambda i: (i, 0)),
            full((FILT_EMB_PAD, FILT_ORDER)),
            full((1, FILT_ORDER)),
            full((FILT_INNER, FILT_ORDER, FILT_ORDER)),
            full((FILT_INNER, FILT_ORDER)),
            full((FILT_ORDER, 2 * HY_WIDTH)),
            full((1, FILT_ORDER)),
        ],
        out_specs=(
            pl.BlockSpec((tl, HY_WIDTH), lambda i: (i, 0)),
            pl.BlockSpec((tl, HY_WIDTH), lambda i: (i, 0)),
        ),
        out_shape=(
            jax.ShapeDtypeStruct((L, HY_WIDTH), F32),
            jax.ShapeDtypeStruct((L, HY_WIDTH), F32),
        ),
        compiler_params=_cparams(("parallel",)),
        name="hy_filter",
    )(z_pad, window, w1_pad, b1.reshape(1, -1), w2, b2, wout, freq.reshape(1, -1))


class _FftPlan:
    def __init__(self, L):
        n = 2 * L
        e = int(round(math.log2(n)))
        assert 2 ** e == n and e >= 8
        self.L, self.n = L, n
        self.n1 = 2 ** (e // 2)
        self.n2 = n // self.n1
        self.h1 = self.n1 // 2
        self.kh = self.h1 + 8
        self.kv = self.h1 + 1
        n1, n2, h1, kh = self.n1, self.n2, self.h1, self.kh
        k1 = np.arange(kh)[:, None]
        a = 2 * np.pi * k1 * np.arange(h1)[None, :] / n1
        self.fa = np.concatenate([np.cos(a), -np.sin(a)], axis=0).astype(np.float32)
        a = 2 * np.pi * np.arange(n2)[:, None] * np.arange(n2)[None, :] / n2
        self.f2r, self.f2i = np.cos(a).astype(np.float32), (-np.sin(a)).astype(np.float32)
        a = 2 * np.pi * k1 * np.arange(n2)[None, :] / n
        self.twr, self.twi = np.cos(a).astype(np.float32), (-np.sin(a)).astype(np.float32)
        wgt = np.where(np.arange(kh) <= h1, 2.0, 0.0)
        wgt[0] = 1.0
        wgt[h1] = 1.0
        a = 2 * np.pi * np.arange(h1)[:, None] * np.arange(kh)[None, :] / n1
        self.e1r = (np.cos(a) * wgt[None, :] / n).astype(np.float32)
        self.e1i = (np.sin(a) * wgt[None, :] / n).astype(np.float32)
        a = 2 * np.pi * np.arange(n2)[:, None] * np.arange(kh)[None, :] / n
        self.e2r, self.e2i = np.cos(a).astype(np.float32), np.sin(a).astype(np.float32)

    def fwd_tables(self):
        return [jnp.asarray(t) for t in (self.fa, self.f2r, self.f2i, self.twr, self.twi)]

    def inv_tables(self):
        return [jnp.asarray(t) for t in (self.e1r, self.e1i, self.e2r, self.e2i)]


def _dft_cols(plan, x_ref, fa_ref, pr_ref, pi_ref):
    fa = fa_ref[...]

    def body(n2, _):
        x = x_ref[pl.ds(n2, plan.h1, stride=plan.n2), :]
        y = jnp.dot(fa, x, precision=HI, preferred_element_type=F32)
        pr_ref[pl.ds(n2, plan.kh, stride=plan.n2), :] = y[:plan.kh]
        pi_ref[pl.ds(n2, plan.kh, stride=plan.n2), :] = y[plan.kh:]
        return 0

    lax.fori_loop(0, plan.n2, body, 0)


def _twiddled_row_dft(plan, k1, f2r, f2i, twr_ref, twi_ref, ar, ai):
    twr = twr_ref[pl.ds(k1, 1), :]
    twi = twi_ref[pl.ds(k1, 1), :]
    mr = f2r * twr - f2i * twi
    mi = f2r * twi + f2i * twr
    m = jnp.concatenate([jnp.concatenate([mr, -mi], axis=1), jnp.concatenate([mi, mr], axis=1)], axis=0)
    y = jnp.dot(m, jnp.concatenate([ar, ai], axis=0), precision=HI, preferred_element_type=F32)
    return y[:plan.n2], y[plan.n2:]


def _hy_spectrum_kernel(plan, kf_ref, kb_ref, fa_ref, f2r_ref, f2i_ref, twr_ref, twi_ref, c_ref, pr_ref, pi_ref):
    f2r, f2i = f2r_ref[...], f2i_ref[...]
    n2 = plan.n2
    for which, src in enumerate((kf_ref, kb_ref)):
        _dft_cols(plan, src, fa_ref, pr_ref, pi_ref)

        def body(k1, _):
            r0 = pl.multiple_of(k1 * n2, n2)
            xr, xi = _twiddled_row_dft(plan, k1, f2r, f2i, twr_ref, twi_ref,
                                       pr_ref[pl.ds(r0, n2), :], pi_ref[pl.ds(r0, n2), :])
            if which == 0:
                c_ref[0, pl.ds(r0, n2), :] = xr
                c_ref[1, pl.ds(r0, n2), :] = xi
            else:
                c_ref[0, pl.ds(r0, n2), :] = c_ref[0, pl.ds(r0, n2), :] + xr
                c_ref[1, pl.ds(r0, n2), :] = c_ref[1, pl.ds(r0, n2), :] - xi
            return 0

        lax.fori_loop(0, plan.kh, body, 0)


def _hy_spectrum(plan, kf, kb):
    rows = plan.kh * plan.n2
    nc = HY_WIDTH // LANES
    tabs = plan.fwd_tables()
    full = lambda a: pl.BlockSpec(a.shape, lambda c: (0,) * a.ndim)
    return pl.pallas_call(
        functools.partial(_hy_spectrum_kernel, plan),
        grid=(nc,),
        in_specs=[pl.BlockSpec((plan.L, LANES), lambda c: (0, c)),
                  pl.BlockSpec((plan.L, LANES), lambda c: (0, c))] + [full(t) for t in tabs],
        out_specs=pl.BlockSpec((2, rows, LANES), lambda c: (0, 0, c)),
        out_shape=jax.ShapeDtypeStruct((2, rows, HY_WIDTH), F32),
        scratch_shapes=[pltpu.VMEM((rows, LANES), F32), pltpu.VMEM((rows, LANES), F32)],
        compiler_params=_cparams(("parallel",)),
        name="hy_spectrum",
    )(kf, kb, *tabs)


def _hy_conv_kernel(plan, hv_ref, c_ref, x0g_ref, bias_ref, fa_ref, f2r_ref, f2i_ref, twr_ref, twi_ref,
                    e1r_ref, e1i_ref, e2r_ref, e2i_ref, o_ref, pr_ref, pi_ref, y_ref):
    n2, kh, h1 = plan.n2, plan.kh, plan.h1
    x_ref = hv_ref.at[0]
    _dft_cols(plan, x_ref, fa_ref, pr_ref, pi_ref)

    f2r, f2i = f2r_ref[...], f2i_ref[...]
    inv = jnp.concatenate([jnp.concatenate([f2r, f2i], axis=1), jnp.concatenate([-f2i, f2r], axis=1)], axis=0)

    def rows_body(k1, _):
        r0 = pl.multiple_of(k1 * n2, n2)
        xr, xi = _twiddled_row_dft(plan, k1, f2r, f2i, twr_ref, twi_ref,
                                   pr_ref[pl.ds(r0, n2), :], pi_ref[pl.ds(r0, n2), :])
        cr, ci = c_ref[0, pl.ds(r0, n2), :], c_ref[1, pl.ds(r0, n2), :]
        z = jnp.concatenate([xr * cr - xi * ci, xr * ci + xi * cr], axis=0)
        b = jnp.dot(inv, z, precision=HI, preferred_element_type=F32)
        pr_ref[pl.ds(r0, n2), :] = b[:n2]
        pi_ref[pl.ds(r0, n2), :] = b[n2:]
        return 0

    lax.fori_loop(0, plan.kv, rows_body, 0)

    e1r, e1i = e1r_ref[...], e1i_ref[...]

    def cols_body(j, _):
        e2r = e2r_ref[pl.ds(j, 1), :]
        e2i = e2i_ref[pl.ds(j, 1), :]
        gr = e1r * e2r - e1i * e2i
        gi = e1r * e2i + e1i * e2r
        g = jnp.concatenate([gr, -gi], axis=1)
        b = jnp.concatenate([pr_ref[pl.ds(j, kh, stride=n2), :], pi_ref[pl.ds(j, kh, stride=n2), :]], axis=0)
        y_ref[pl.ds(j, h1, stride=n2), :] = jnp.dot(g, b, precision=HI, preferred_element_type=F32)
        return 0

    lax.fori_loop(0, n2, cols_body, 0)

    bias = bias_ref[...]
    chunk = min(plan.L, 512)

    def out_body(t, _):
        r0 = pl.multiple_of(t * chunk, chunk)
        hv = hv_ref[0, pl.ds(r0, chunk), :]
        y = y_ref[pl.ds(r0, chunk), :] + hv * bias
        o_ref[0, pl.ds(r0, chunk), :] = (y * x0g_ref[0, pl.ds(r0, chunk), :].astype(F32)).astype(o_ref.dtype)
        return 0

    lax.fori_loop(0, plan.L // chunk, out_body, 0)


def _hy_conv(plan, hv, spec, x0g, hy_bias):
    B, L, _ = hv.shape
    rows = plan.kh * plan.n2
    nc = HY_WIDTH // LANES
    tabs = plan.fwd_tables() + plan.inv_tables()
    full = lambda a: pl.BlockSpec(a.shape, lambda c, b: (0,) * a.ndim)
    seq = pl.BlockSpec((1, L, LANES), lambda c, b: (b, 0, c))
    return pl.pallas_call(
        functools.partial(_hy_conv_kernel, plan),
        grid=(nc, B),
        in_specs=[seq,
                  pl.BlockSpec((2, rows, LANES), lambda c, b: (0, 0, c), pipeline_mode=pl.Buffered(1)),
                  seq,
                  pl.BlockSpec((1, LANES), lambda c, b: (0, c))] + [full(t) for t in tabs],
        out_specs=seq,
        out_shape=jax.ShapeDtypeStruct((B, L, HY_WIDTH), BF16),
        scratch_shapes=[pltpu.VMEM((rows, LANES), F32), pltpu.VMEM((rows, LANES), F32),
                        pltpu.VMEM((L, LANES), F32)],
        compiler_params=_cparams(("parallel", "parallel")),
        name="hy_conv",
    )(hv, spec, x0g, hy_bias.reshape(1, HY_WIDTH), *tabs)


def _gqa_kernel(tk, q0_ref, q1_ref, q2_ref, q3_ref, k_ref, v_ref, sg_ref, o_ref, m_ref, l_ref, acc_ref):
    qt = jnp.concatenate([q0_ref[0, 0], q1_ref[0, 0], q2_ref[0, 0], q3_ref[0, 0]], axis=1)
    tq = q0_ref.shape[-1]
    L = k_ref.shape[2]
    m_ref[...] = jnp.full(m_ref.shape, -jnp.inf, F32)
    l_ref[...] = jnp.zeros(l_ref.shape, F32)
    acc_ref[...] = jnp.zeros(acc_ref.shape, F32)

    def body(j, _):
        r0 = pl.multiple_of(j * tk, tk)
        s = _dotf(k_ref[0, 0, pl.ds(r0, tk), :], qt)
        m_old = m_ref[...]
        m_new = jnp.maximum(m_old, jnp.max(s, axis=0, keepdims=True))
        alpha = jnp.exp(m_old - m_new)
        p = jnp.exp(s - m_new)
        l_ref[...] = alpha * l_ref[...] + jnp.sum(p, axis=0, keepdims=True)
        acc_ref[...] = alpha * acc_ref[...] + _dotf(v_ref[0, 0, :, pl.ds(r0, tk)], p.astype(BF16))
        m_ref[...] = m_new
        return 0

    lax.fori_loop(0, L // tk, body, 0)
    o = acc_ref[...] * (1.0 / l_ref[...])
    ot = jnp.concatenate([o[:, h * tq:(h + 1) * tq].T for h in range(GQA_GROUP)], axis=1)
    o_ref[0] = (ot * sg_ref[0].astype(F32)).astype(o_ref.dtype)


def _gqa_attn(qT, k, vT, sg, tq, tk):
    B, _, _, L = qT.shape
    gw = GQA_GROUP * HEAD_DIM
    qspec = lambda hh: pl.BlockSpec((1, 1, HEAD_DIM, tq), lambda b, g, i: (b, GQA_GROUP * g + hh, 0, i))
    return pl.pallas_call(
        functools.partial(_gqa_kernel, tk),
        grid=(B, GQA_KV_HEADS, L // tq),
        in_specs=[qspec(0), qspec(1), qspec(2), qspec(3),
                  pl.BlockSpec((1, 1, L, HEAD_DIM), lambda b, g, i: (b, g, 0, 0)),
                  pl.BlockSpec((1, 1, HEAD_DIM, L), lambda b, g, i: (b, g, 0, 0)),
                  pl.BlockSpec((1, tq, gw), lambda b, g, i: (b, i, g))],
        out_specs=pl.BlockSpec((1, tq, gw), lambda b, g, i: (b, i, g)),
        out_shape=jax.ShapeDtypeStruct((B, L, GQA_HEADS * HEAD_DIM), BF16),
        scratch_shapes=[pltpu.VMEM((1, GQA_GROUP * tq), F32), pltpu.VMEM((1, GQA_GROUP * tq), F32),
                        pltpu.VMEM((HEAD_DIM, GQA_GROUP * tq), F32)],
        compiler_params=_cparams(("parallel", "parallel", "parallel")),
        name="gqa_attn",
    )(qT, qT, qT, qT, k, vT, sg)


def _t5_bias_kernel(t, rb_ref, o_ref):
    h = pl.program_id(0)
    kk = lax.broadcasted_iota(jnp.int32, (t, t), 0)
    qq = lax.broadcasted_iota(jnp.int32, (t, t), 1)
    nb = REL_BUCKETS // 2
    max_exact = nb // 2
    thresholds = [int(math.ceil(max_exact * (REL_MAX_DIST / max_exact) ** (j / (nb - max_exact)) - 1e-9))
                  for j in range(1, nb - max_exact)]
    for idx in range(5):
        rel = (idx - 2) * t + kk - qq
        n = jnp.abs(rel)
        large = jnp.full((t, t), max_exact, jnp.int32)
        for th in thresholds:
            large = large + (n >= th).astype(jnp.int32)
        bucket = jnp.where(rel > 0, nb, 0) + jnp.where(n < max_exact, n, large)
        val = jnp.zeros((t, t), F32)
        for b in range(REL_BUCKETS):
            val = jnp.where(bucket == b, rb_ref[h, b], val)
        o_ref[0, idx] = val


def _t5_bias_tiles(rel_bias, t):
    return pl.pallas_call(
        functools.partial(_t5_bias_kernel, t),
        grid=(DIFF_HEADS,),
        in_specs=[pl.BlockSpec(memory_space=pltpu.SMEM)],
        out_specs=pl.BlockSpec((1, 5, t, t), lambda h: (h, 0, 0, 0)),
        out_shape=jax.ShapeDtypeStruct((DIFF_HEADS, 5, t, t), F32),
        compiler_params=_cparams(("parallel",)),
        name="t5_bias_tiles",
    )(rel_bias.T)


def _diff_kernel(t, lam_init, q_ref, k0_ref, k1_ref, v_ref, bias_ref, lam_ref, g_ref, sg_ref, o_ref,
                 m_ref, l_ref, acc_ref):
    i = pl.program_id(2)
    L = k0_ref.shape[2]
    qt = q_ref[0, 0]
    m_ref[...] = jnp.full(m_ref.shape, -jnp.inf, F32)
    l_ref[...] = jnp.zeros(l_ref.shape, F32)
    acc_ref[...] = jnp.zeros(acc_ref.shape, F32)

    def body(j, _):
        r0 = pl.multiple_of(j * t, t)
        bias = bias_ref[0, jnp.clip(j - i, -2, 2) + 2]
        s = jnp.concatenate(
            [_dotf(k0_ref[0, 0, pl.ds(r0, t), :], qt[:HEAD_DIM]) + bias,
             _dotf(k1_ref[0, 0, pl.ds(r0, t), :], qt[HEAD_DIM:]) + bias], axis=1)
        m_old = m_ref[...]
        m_new = jnp.maximum(m_old, jnp.max(s, axis=0, keepdims=True))
        alpha = jnp.exp(m_old - m_new)
        p = jnp.exp(s - m_new)
        l_ref[...] = alpha * l_ref[...] + jnp.sum(p, axis=0, keepdims=True)
        acc_ref[...] = alpha * acc_ref[...] + _dotf(v_ref[0, 0, :, pl.ds(r0, t)], p.astype(BF16))
        m_ref[...] = m_new
        return 0

    lax.fori_loop(0, L // t, body, 0)

    lp = lam_ref[...]
    lam = (jnp.exp(jnp.sum(lp[0:1] * lp[1:2], axis=1, keepdims=True))
           - jnp.exp(jnp.sum(lp[2:3] * lp[3:4], axis=1, keepdims=True)) + lam_init)
    o = acc_ref[...] * (1.0 / l_ref[...])
    o = o[:, :t] - lam * o[:, t:]
    ms = jnp.mean(o * o, axis=0, keepdims=True)
    o = o * lax.rsqrt(ms + EPS) * g_ref[...] * (1.0 - lam_init)
    o_ref[0] = (o.T * sg_ref[0].astype(F32)).astype(o_ref.dtype)


def _diff_attn(qT, k, vT, bias_tiles, lam_params, subln_g, sg, lam_init, t):
    B, _, _, L = qT.shape
    hw = 2 * HEAD_DIM
    g_tab = jnp.broadcast_to(subln_g[:, None], (hw, t))
    kspec = lambda c: pl.BlockSpec((1, 1, L, HEAD_DIM), lambda b, h, i: (b, 2 * h + c, 0, 0))
    return pl.pallas_call(
        functools.partial(_diff_kernel, t, lam_init),
        grid=(B, DIFF_HEADS, L // t),
        in_specs=[pl.BlockSpec((1, 1, hw, t), lambda b, h, i: (b, h, 0, i)),
                  kspec(0), kspec(1),
                  pl.BlockSpec((1, 1, hw, L), lambda b, h, i: (b, h, 0, 0)),
                  pl.BlockSpec((1, 5, t, t), lambda b, h, i: (h, 0, 0, 0)),
                  pl.BlockSpec((4, HEAD_DIM), lambda b, h, i: (0, 0)),
                  pl.BlockSpec((hw, t), lambda b, h, i: (0, 0)),
                  pl.BlockSpec((1, t, hw), lambda b, h, i: (b, i, h))],
        out_specs=pl.BlockSpec((1, t, hw), lambda b, h, i: (b, i, h)),
        out_shape=jax.ShapeDtypeStruct((B, L, DIFF_HEADS * hw), BF16),
        scratch_shapes=[pltpu.VMEM((1, 2 * t), F32), pltpu.VMEM((1, 2 * t), F32),
                        pltpu.VMEM((hw, 2 * t), F32)],
        compiler_params=_cparams(("parallel", "parallel", "parallel")),
        name="diff_attn",
    )(qT, k, k, vT, bias_tiles, lam_params, g_tab, sg)


def _merge_kernel(final, x_ref, ng_ref, yh_ref, yg_ref, yd_ref, wm_ref, bm_ref, wh_ref, wg_ref, wd_ref,
                  wo_ref, fg_ref, o_ref):
    x = x_ref[...]
    ms = jnp.mean(x * x, axis=-1, keepdims=True)
    h = (x * lax.rsqrt(ms + EPS) * ng_ref[...]).astype(BF16)
    merged = None
    for b, (y_ref, w_ref) in enumerate(((yh_ref, wh_ref), (yg_ref, wg_ref), (yd_ref, wd_ref))):
        z = _dotf(h, wm_ref[:, b * D_MODEL:(b + 1) * D_MODEL]) + bm_ref[:, b * D_MODEL:(b + 1) * D_MODEL]
        gate = 1.0 / (1.0 + jnp.exp(-z))
        term = gate * _dotf(y_ref[...], w_ref[...])
        merged = term if merged is None else merged + term
    y = x + _dotf(merged.astype(BF16), wo_ref[...])
    if final:
        ms = jnp.mean(y * y, axis=-1, keepdims=True)
        y = y * lax.rsqrt(ms + EPS) * fg_ref[...]
    o_ref[...] = y


def _merge_out(x2, norm_g, y_hy, y_gq, y_df, w_merge, b_merge, w_hy, w_gq, w_df, w_out, final_g, final, tm):
    T = x2.shape[0]
    full = lambda a: pl.BlockSpec(a.shape, lambda i: (0,) * a.ndim)
    tok = lambda w: pl.BlockSpec((tm, w), lambda i: (i, 0))
    ng = norm_g.reshape(1, D_MODEL)
    bm = b_merge.reshape(1, -1)
    fg = final_g.reshape(1, D_MODEL)
    return pl.pallas_call(
        functools.partial(_merge_kernel, final),
        grid=(T // tm,),
        in_specs=[tok(D_MODEL), full(ng), tok(HY_WIDTH), tok(GQA_HEADS * HEAD_DIM), tok(DIFF_HEADS * 2 * HEAD_DIM),
                  full(w_merge), full(bm), full(w_hy), full(w_gq), full(w_df), full(w_out), full(fg)],
        out_specs=tok(D_MODEL),
        out_shape=jax.ShapeDtypeStruct((T, D_MODEL), F32),
        compiler_params=_cparams(("parallel",)),
        name="merge_out",
    )(x2, ng, y_hy, y_gq, y_df, w_merge, bm, w_hy, w_gq, w_df, w_out, fg)


def _rope_tables_t(L):
    rows = L // GRID_W
    row = jnp.broadcast_to(jnp.arange(rows, dtype=F32)[:, None], (rows, GRID_W)).reshape(L)
    col = jnp.broadcast_to(jnp.arange(GRID_W, dtype=F32)[None, :], (rows, GRID_W)).reshape(L)
    n_freq = HEAD_DIM // 4
    inv_freq = ROPE_THETA ** (-jnp.arange(n_freq, dtype=F32) / n_freq)
    ang = jnp.concatenate([row[:, None] * inv_freq, col[:, None] * inv_freq], axis=-1)
    return jnp.cos(ang).T, jnp.sin(ang).T


def _hyena_positions(L):
    t01 = jnp.linspace(0.0, 1.0, L, dtype=F32)[:, None]
    bands = (FILT_EMB - 1) // 2
    w = 2.0 * math.pi * jnp.arange(L, dtype=F32)[:, None] / L
    f = jnp.linspace(1e-4, bands - 1, bands, dtype=F32)[None, :]
    z = jnp.concatenate([t01, jnp.cos(f * w), -jnp.sin(f * w)], axis=-1)
    z = jnp.pad(z, ((0, 0), (0, FILT_EMB_PAD - FILT_EMB)))
    max_decay = math.log(HY_TARGET) / HY_FAST_DECAY
    min_decay = math.log(HY_TARGET) / HY_SLOW_DECAY
    deltas = jnp.linspace(min_decay, max_decay, HY_WIDTH, dtype=F32)
    window = jnp.exp(-t01 * jnp.abs(deltas)[None, :]) + HY_SHIFT
    return z, window


def _tile(L, want):
    t = min(L, want)
    assert L % t == 0
    return t


def _trunk(x, p, bias_tiles, attn_t):
    B, L, _ = x.shape
    depth = p['w_in'].shape[0]
    cos_t, sin_t = _rope_tables_t(L)
    z_pad, window = _hyena_positions(L)
    plan = _FftPlan(L)
    tm = _tile(L, 256)
    x2 = x.reshape(B * L, D_MODEL)
    for l in range(depth):
        (hy_u, hy_sg, gq_qT, gq_k, gq_vT, gq_sg, df_qT, df_k, df_vT, df_sg) = _in_proj(
            x2.reshape(B, L, D_MODEL), p['norm_g'][l], p['w_in_bf'][l], p['q_norm_g'][l], p['k_norm_g'][l],
            cos_t, sin_t, tm)

        hv, x0g = _hy_pre(hy_u, p['hy_conv_w'][l], p['hy_conv_b'][l], hy_sg, _tile(L, 512))
        w1_pad = jnp.pad(p['hy_f_w1'][l], ((0, FILT_EMB_PAD - FILT_EMB), (0, 0)))
        kf, kb = _hy_filter(z_pad, window, w1_pad, p['hy_f_b1'][l], p['hy_f_w2'][l], p['hy_f_b2'][l],
                            p['hy_f_wout'][l], p['hy_f_freq'][l], _tile(L, 512))
        spec = _hy_spectrum(plan, kf, kb)
        y_hy = _hy_conv(plan, hv, spec, x0g, p['hy_bias'][l])

        y_gq = _gqa_attn(gq_qT, gq_k, gq_vT, gq_sg, _tile(L, 128), _tile(L, 512))

        lam_init = 0.8 - 0.6 * math.exp(-0.3 * l)
        lam_params = jnp.stack([p['lam_q1'][l], p['lam_k1'][l], p['lam_q2'][l], p['lam_k2'][l]])
        y_df = _diff_attn(df_qT, df_k, df_vT, bias_tiles, lam_params, p['diff_subln_g'][l], df_sg, lam_init, attn_t)

        x2 = _merge_out(x2, p['norm_g'][l], y_hy.reshape(B * L, -1), y_gq.reshape(B * L, -1),
                        y_df.reshape(B * L, -1), p['w_merge_bf'][l], p['b_merge'][l], p['w_branch_hy_bf'][l],
                        p['w_branch_gqa_bf'][l], p['w_branch_diff_bf'][l], p['w_out_bf'][l], p['final_g'],
                        l == depth - 1, _tile(B * L, 512))
    return x2.reshape(B, L, D_MODEL)


def kernel(x_prompt, x_sample, rel_bias, norm_g, w_in, hy_conv_w, hy_conv_b, hy_f_w1, hy_f_b1, hy_f_w2, hy_f_b2, hy_f_wout, hy_f_freq, hy_bias, q_norm_g, k_norm_g, lam_q1, lam_k1, lam_q2, lam_k2, diff_subln_g, w_branch_hy, w_branch_gqa, w_branch_diff, w_merge, b_merge, w_out, final_g):
    p = dict(norm_g=norm_g, hy_conv_w=hy_conv_w, hy_conv_b=hy_conv_b, hy_f_w1=hy_f_w1,
             hy_f_b1=hy_f_b1, hy_f_w2=hy_f_w2, hy_f_b2=hy_f_b2, hy_f_wout=hy_f_wout, hy_f_freq=hy_f_freq,
             hy_bias=hy_bias, q_norm_g=q_norm_g, k_norm_g=k_norm_g, lam_q1=lam_q1, lam_k1=lam_k1,
             lam_q2=lam_q2, lam_k2=lam_k2, diff_subln_g=diff_subln_g, b_merge=b_merge, final_g=final_g,
             w_in=w_in,
             w_in_bf=w_in.astype(BF16), w_merge_bf=w_merge.astype(BF16), w_out_bf=w_out.astype(BF16),
             w_branch_hy_bf=w_branch_hy.astype(BF16), w_branch_gqa_bf=w_branch_gqa.astype(BF16),
             w_branch_diff_bf=w_branch_diff.astype(BF16))
    outs = []
    for x in (x_prompt, x_sample):
        attn_t = _tile(x.shape[1], 256)
        bias_tiles = _t5_bias_tiles(rel_bias, attn_t)
        outs.append(_trunk(x, p, bias_tiles, attn_t))
    return tuple(outs)
```

```python
import functools
import math

import numpy as np
import jax
import jax.numpy as jnp
from jax import lax
from jax.experimental import pallas as pl
from jax.experimental.pallas import tpu as pltpu

D_MODEL = 1024
HEAD_DIM = 64
EPS = 1e-6
GRID_W = 64
ROPE_THETA = 10000.0

HY_WIDTH = 512
FILT_EMB = 33
FILT_EMB_PAD = 64
FILT_ORDER = 64
FILT_INNER = 2
HY_FAST_DECAY = 0.3
HY_SLOW_DECAY = 1.5
HY_TARGET = 1e-2
HY_SHIFT = 0.0

GQA_HEADS = 8
GQA_KV_HEADS = 2
GQA_GROUP = GQA_HEADS // GQA_KV_HEADS
DIFF_HEADS = 4
REL_BUCKETS = 32
REL_MAX_DIST = 128

C_HY_U = 0
C_HY_G = 1536
C_GQ_Q = 2048
C_GQ_K = 2560
C_GQ_V = 2688
C_GQ_G = 2816
C_DF_Q = 3328
C_DF_K = 3840
C_DF_V = 4352
C_DF_G = 4864
IN_COLS = 5376

LANES = 128
ONES_ROWS = 16
VMEM_LIMIT = 56 * 1024 * 1024
LOG2E = math.log2(math.e)
HI = lax.Precision.HIGHEST
F32 = jnp.float32
BF16 = jnp.bfloat16


def _cparams(sem):
    return pltpu.CompilerParams(dimension_semantics=sem, vmem_limit_bytes=VMEM_LIMIT)


def _silu(x):
    return x * (1.0 / (1.0 + jnp.exp(-x)))


def _dotf(a, b):
    return jnp.dot(a, b, preferred_element_type=F32)


def _norm_rope_t(xt, g_tab, cos, sin):
    ms = jnp.mean(xt * xt, axis=1, keepdims=True)
    xn = xt * lax.rsqrt(ms + EPS) * g_tab[None]
    half = HEAD_DIM // 2
    x1, x2 = xn[:, :half, :], xn[:, half:, :]
    c, s = cos[None], sin[None]
    return jnp.concatenate([x1 * c - x2 * s, x2 * c + x1 * s], axis=1)


def _in_proj_kernel(x_ref, ng_ref, w_ref, qg_ref, kg_ref, cos_ref, sin_ref,
                    hyu_ref, hysg_ref, gqq_ref, gqk_ref, gqv_ref, gqsg_ref,
                    dfq_ref, dfk_ref, dfv_ref, dfsg_ref):
    x = x_ref[0]
    tm = x.shape[0]
    ms = jnp.mean(x * x, axis=-1, keepdims=True)
    h = (x * lax.rsqrt(ms + EPS) * ng_ref[...]).astype(BF16)

    def proj(lo, hi):
        return _dotf(h, w_ref[:, lo:hi])

    hyu_ref[0] = proj(C_HY_U, C_HY_G)
    hysg_ref[0] = _silu(proj(C_HY_G, C_GQ_Q)).astype(BF16)

    cos, sin = cos_ref[...], sin_ref[...]
    scale = HEAD_DIM ** -0.5
    qt = proj(C_GQ_Q, C_GQ_K).T.reshape(GQA_HEADS, HEAD_DIM, tm)
    gqq_ref[0] = (_norm_rope_t(qt, qg_ref[...], cos, sin) * (scale * LOG2E)).astype(BF16)
    kt = proj(C_GQ_K, C_GQ_V).T.reshape(GQA_KV_HEADS, HEAD_DIM, tm)
    kt = _norm_rope_t(kt, kg_ref[...], cos, sin)
    for g in range(GQA_KV_HEADS):
        gqk_ref[0, g] = kt[g].T.astype(BF16)
    ones_rows = (lax.broadcasted_iota(jnp.int32, (ONES_ROWS, tm), 0) == 0).astype(BF16)
    vt = proj(C_GQ_V, C_GQ_G).T.astype(BF16)
    for g in range(GQA_KV_HEADS):
        gqv_ref[0, g, :HEAD_DIM, :] = vt[g * HEAD_DIM:(g + 1) * HEAD_DIM]
        gqv_ref[0, g, HEAD_DIM:, :] = ones_rows
    gqsg_ref[0] = _silu(proj(C_GQ_G, C_DF_Q)).astype(BF16)

    dfq_ref[0] = (proj(C_DF_Q, C_DF_K) * (scale * LOG2E)).T.reshape(DIFF_HEADS, 2 * HEAD_DIM, tm).astype(BF16)
    dk = proj(C_DF_K, C_DF_V).astype(BF16)
    for j in range(2 * DIFF_HEADS):
        dfk_ref[0, j] = dk[:, j * HEAD_DIM:(j + 1) * HEAD_DIM]
    dvt = proj(C_DF_V, C_DF_G).T.astype(BF16)
    for hh in range(DIFF_HEADS):
        dfv_ref[0, hh, :2 * HEAD_DIM, :] = dvt[hh * 2 * HEAD_DIM:(hh + 1) * 2 * HEAD_DIM]
        dfv_ref[0, hh, 2 * HEAD_DIM:, :] = ones_rows
    dfsg_ref[0] = _silu(proj(C_DF_G, IN_COLS)).astype(BF16)


def _in_proj(x, norm_g, w_in_bf, q_g, k_g, cos_t, sin_t, tm):
    B, L, _ = x.shape
    nt = L // tm
    qg_tab = jnp.broadcast_to(q_g[:, None], (HEAD_DIM, tm))
    kg_tab = jnp.broadcast_to(k_g[:, None], (HEAD_DIM, tm))
    full = lambda shape: pl.BlockSpec(shape, lambda b, i: (0,) * len(shape))
    out_shapes = (
        jax.ShapeDtypeStruct((B, L, 3 * HY_WIDTH), F32),
        jax.ShapeDtypeStruct((B, L, HY_WIDTH), BF16),
        jax.ShapeDtypeStruct((B, GQA_HEADS, HEAD_DIM, L), BF16),
        jax.ShapeDtypeStruct((B, GQA_KV_HEADS, L, HEAD_DIM), BF16),
        jax.ShapeDtypeStruct((B, GQA_KV_HEADS, HEAD_DIM + ONES_ROWS, L), BF16),
        jax.ShapeDtypeStruct((B, L, GQA_HEADS * HEAD_DIM), BF16),
        jax.ShapeDtypeStruct((B, DIFF_HEADS, 2 * HEAD_DIM, L), BF16),
        jax.ShapeDtypeStruct((B, 2 * DIFF_HEADS, L, HEAD_DIM), BF16),
        jax.ShapeDtypeStruct((B, DIFF_HEADS, 2 * HEAD_DIM + ONES_ROWS, L), BF16),
        jax.ShapeDtypeStruct((B, L, DIFF_HEADS * 2 * HEAD_DIM), BF16),
    )
    tok = lambda w: pl.BlockSpec((1, tm, w), lambda b, i: (b, i, 0))
    tr = lambda h, d: pl.BlockSpec((1, h, d, tm), lambda b, i: (b, 0, 0, i))
    rows = lambda h: pl.BlockSpec((1, h, tm, HEAD_DIM), lambda b, i: (b, 0, i, 0))
    return pl.pallas_call(
        _in_proj_kernel,
        grid=(B, nt),
        in_specs=[
            tok(D_MODEL),
            full((1, D_MODEL)),
            full((D_MODEL, IN_COLS)),
            full((HEAD_DIM, tm)),
            full((HEAD_DIM, tm)),
            pl.BlockSpec((HEAD_DIM // 2, tm), lambda b, i: (0, i)),
            pl.BlockSpec((HEAD_DIM // 2, tm), lambda b, i: (0, i)),
        ],
        out_specs=(
            tok(3 * HY_WIDTH), tok(HY_WIDTH),
            tr(GQA_HEADS, HEAD_DIM), rows(GQA_KV_HEADS), tr(GQA_KV_HEADS, HEAD_DIM + ONES_ROWS),
            tok(GQA_HEADS * HEAD_DIM),
            tr(DIFF_HEADS, 2 * HEAD_DIM), rows(2 * DIFF_HEADS), tr(DIFF_HEADS, 2 * HEAD_DIM + ONES_ROWS),
            tok(DIFF_HEADS * 2 * HEAD_DIM),
        ),
        out_shape=out_shapes,
        compiler_params=_cparams(("parallel", "parallel")),
        name="in_proj",
    )(x, norm_g.reshape(1, D_MODEL), w_in_bf, qg_tab, kg_tab, cos_t, sin_t)


def _hy_pre_kernel(u_ref, prev_ref, next_ref, w_ref, b_ref, sg_ref, hv_ref, x0g_ref):
    i = pl.program_id(1)
    nt = pl.num_programs(1)
    u = u_ref[0]
    tl = u.shape[0]
    prev_row = jnp.where(i > 0, prev_ref[0, 0, 7:8, :], 0.0)
    next_row = jnp.where(i < nt - 1, next_ref[0, 0, 0:1, :], 0.0)
    row = lax.broadcasted_iota(jnp.int32, u.shape, 0)
    up = jnp.where(row == 0, prev_row, pltpu.roll(u, 1, 0))
    dn = jnp.where(row == tl - 1, next_row, pltpu.roll(u, tl - 1, 0))
    w = w_ref[...]
    hy = up * w[0:1] + u * w[1:2] + dn * w[2:3] + b_ref[...]
    x0 = hy[:, :HY_WIDTH]
    x1 = hy[:, HY_WIDTH:2 * HY_WIDTH]
    hv = hy[:, 2 * HY_WIDTH:]
    hv_ref[0] = hv * x1
    x0g_ref[0] = (x0 * sg_ref[0].astype(F32)).astype(x0g_ref.dtype)


def _hy_pre(hy_u, conv_w, conv_b, hy_sg, tl):
    B, L, W3 = hy_u.shape
    nt = L // tl
    g8 = tl // 8
    u4 = hy_u.reshape(B, L // 8, 8, W3)
    return pl.pallas_call(
        _hy_pre_kernel,
        grid=(B, nt),
        in_specs=[
            pl.BlockSpec((1, tl, W3), lambda b, i: (b, i, 0)),
            pl.BlockSpec((1, 1, 8, W3), lambda b, i: (b, jnp.maximum(i * g8 - 1, 0), 0, 0)),
            pl.BlockSpec((1, 1, 8, W3), lambda b, i: (b, jnp.minimum((i + 1) * g8, L // 8 - 1), 0, 0)),
            pl.BlockSpec((3, W3), lambda b, i: (0, 0)),
            pl.BlockSpec((1, W3), lambda b, i: (0, 0)),
            pl.BlockSpec((1, tl, HY_WIDTH), lambda b, i: (b, i, 0)),
        ],
        out_specs=(
            pl.BlockSpec((1, tl, HY_WIDTH), lambda b, i: (b, i, 0)),
            pl.BlockSpec((1, tl, HY_WIDTH), lambda b, i: (b, i, 0)),
        ),
        out_shape=(
            jax.ShapeDtypeStruct((B, L, HY_WIDTH), F32),
            jax.ShapeDtypeStruct((B, L, HY_WIDTH), BF16),
        ),
        compiler_params=_cparams(("parallel", "parallel")),
        name="hy_pre",
    )(hy_u, u4, u4, conv_w, conv_b.reshape(1, W3), hy_sg)


def _hy_filter_kernel(z_ref, win_ref, w1_ref, b1_ref, w2_ref, b2_ref, wout_ref, freq_ref, kf_ref, kb_ref):
    i = pl.program_id(0)
    freq = freq_ref[...]
    a = jnp.sin(freq * (jnp.dot(z_ref[...], w1_ref[...], precision=HI, preferred_element_type=F32) + b1_ref[...]))
    for j in range(FILT_INNER):
        a = jnp.sin(freq * (jnp.dot(a, w2_ref[j], precision=HI, preferred_element_type=F32) + b2_ref[j:j + 1, :]))
    hf = jnp.dot(a, wout_ref[...], precision=HI, preferred_element_type=F32)
    win = win_ref[...]
    kf_ref[...] = hf[:, :HY_WIDTH] * win
    row = lax.broadcasted_iota(jnp.int32, win.shape, 0)
    kb_ref[...] = jnp.where((row == 0) & (i == 0), 0.0, hf[:, HY_WIDTH:] * win)


def _hy_filter(z_pad, window, w1_pad, b1, w2, b2, wout, freq, tl):
    L = z_pad.shape[0]
    full = lambda shape: pl.BlockSpec(shape, lambda i: (0,) * len(shape))
    return pl.pallas_call(
        _hy_filter_kernel,
        grid=(L // tl,),
        in_specs=[
            pl.BlockSpec((tl, FILT_EMB_PAD), lambda i: (i, 0)),
            pl.BlockSpec((tl, HY_WIDTH), lambda i: (i, 0)),
            full((FILT_EMB_PAD, FILT_ORDER)),
            full((1, FILT_ORDER)),
            full((FILT_INNER, FILT_ORDER, FILT_ORDER)),
            full((FILT_INNER, FILT_ORDER)),
            full((FILT_ORDER, 2 * HY_WIDTH)),
            full((1, FILT_ORDER)),
        ],
        out_specs=(
            pl.BlockSpec((tl, HY_WIDTH), lambda i: (i, 0)),
            pl.BlockSpec((tl, HY_WIDTH), lambda i: (i, 0)),
        ),
        out_shape=(
            jax.ShapeDtypeStruct((L, HY_WIDTH), F32),
            jax.ShapeDtypeStruct((L, HY_WIDTH), F32),
        ),
        compiler_params=_cparams(("parallel",)),
        name="hy_filter",
    )(z_pad, window, w1_pad, b1.reshape(1, -1), w2, b2, wout, freq.reshape(1, -1))


class _FftPlan:
    def __init__(self, L):
        n = 2 * L
        e = int(round(math.log2(n)))
        assert 2 ** e == n and e >= 8
        self.L, self.n = L, n
        self.n1 = 2 ** (e // 2)
        self.n2 = n // self.n1
        self.h1 = self.n1 // 2
        self.kh = self.h1 + 8
        self.kv = self.h1 + 1
        n1, n2, h1, kh = self.n1, self.n2, self.h1, self.kh
        k1 = np.arange(kh)[:, None]
        a = 2 * np.pi * k1 * np.arange(h1)[None, :] / n1
        self.fa = np.concatenate([np.cos(a), -np.sin(a)], axis=0).astype(np.float32)
        a = 2 * np.pi * np.arange(n2)[:, None] * np.arange(n2)[None, :] / n2
        self.f2r, self.f2i = np.cos(a).astype(np.float32), (-np.sin(a)).astype(np.float32)
        a = 2 * np.pi * k1 * np.arange(n2)[None, :] / n
        self.twr, self.twi = np.cos(a).astype(np.float32), (-np.sin(a)).astype(np.float32)
        wgt = np.where(np.arange(kh) <= h1, 2.0, 0.0)
        wgt[0] = 1.0
        wgt[h1] = 1.0
        a = 2 * np.pi * np.arange(h1)[:, None] * np.arange(kh)[None, :] / n1
        self.e1r = (np.cos(a) * wgt[None, :] / n).astype(np.float32)
        self.e1i = (np.sin(a) * wgt[None, :] / n).astype(np.float32)
        a = 2 * np.pi * np.arange(n2)[:, None] * np.arange(kh)[None, :] / n
        self.e2r, self.e2i = np.cos(a).astype(np.float32), np.sin(a).astype(np.float32)

    def fwd_tables(self):
        return [jnp.asarray(t) for t in (self.fa, self.f2r, self.f2i, self.twr, self.twi)]

    def inv_tables(self):
        return [jnp.asarray(t) for t in (self.e1r, self.e1i, self.e2r, self.e2i)]


def _dot_exact(a, b):
    return jnp.dot(a, b, precision=HI, preferred_element_type=F32)


def _dot_fast(a, b):
    return _dotf(a.astype(BF16), b.astype(BF16))


def _dft_cols(plan, x_ref, fa_ref, pr_ref, pi_ref, dot):
    fa = fa_ref[...]

    def body(n2, _):
        x = x_ref[pl.ds(n2, plan.h1, stride=plan.n2), :]
        y = dot(fa, x)
        pr_ref[pl.ds(n2, plan.kh, stride=plan.n2), :] = y[:plan.kh]
        pi_ref[pl.ds(n2, plan.kh, stride=plan.n2), :] = y[plan.kh:]
        return 0

    lax.fori_loop(0, plan.n2, body, 0, unroll=4)


def _twiddled_row_dft(plan, k1, f2r, f2i, twr_ref, twi_ref, ar, ai, dot):
    twr = twr_ref[pl.ds(k1, 1), :]
    twi = twi_ref[pl.ds(k1, 1), :]
    mr = f2r * twr - f2i * twi
    mi = f2r * twi + f2i * twr
    m = jnp.concatenate([jnp.concatenate([mr, -mi], axis=1), jnp.concatenate([mi, mr], axis=1)], axis=0)
    y = dot(m, jnp.concatenate([ar, ai], axis=0))
    return y[:plan.n2], y[plan.n2:]


def _hy_spectrum_kernel(plan, kf_ref, kb_ref, fa_ref, f2r_ref, f2i_ref, twr_ref, twi_ref, c_ref, pr_ref, pi_ref):
    f2r, f2i = f2r_ref[...], f2i_ref[...]
    n2 = plan.n2
    for which, src in enumerate((kf_ref, kb_ref)):
        _dft_cols(plan, src, fa_ref, pr_ref, pi_ref, _dot_exact)

        def body(k1, _):
            r0 = pl.multiple_of(k1 * n2, n2)
            xr, xi = _twiddled_row_dft(plan, k1, f2r, f2i, twr_ref, twi_ref,
                                       pr_ref[pl.ds(r0, n2), :], pi_ref[pl.ds(r0, n2), :], _dot_exact)
            if which == 0:
                c_ref[0, pl.ds(r0, n2), :] = xr
                c_ref[1, pl.ds(r0, n2), :] = xi
            else:
                c_ref[0, pl.ds(r0, n2), :] = c_ref[0, pl.ds(r0, n2), :] + xr
                c_ref[1, pl.ds(r0, n2), :] = c_ref[1, pl.ds(r0, n2), :] - xi
            return 0

        lax.fori_loop(0, plan.kh, body, 0, unroll=2)


def _hy_spectrum(plan, kf, kb):
    rows = plan.kh * plan.n2
    nc = HY_WIDTH // LANES
    tabs = plan.fwd_tables()
    full = lambda a: pl.BlockSpec(a.shape, lambda c: (0,) * a.ndim)
    return pl.pallas_call(
        functools.partial(_hy_spectrum_kernel, plan),
        grid=(nc,),
        in_specs=[pl.BlockSpec((plan.L, LANES), lambda c: (0, c)),
                  pl.BlockSpec((plan.L, LANES), lambda c: (0, c))] + [full(t) for t in tabs],
        out_specs=pl.BlockSpec((2, rows, LANES), lambda c: (0, 0, c)),
        out_shape=jax.ShapeDtypeStruct((2, rows, HY_WIDTH), F32),
        scratch_shapes=[pltpu.VMEM((rows, LANES), F32), pltpu.VMEM((rows, LANES), F32)],
        compiler_params=_cparams(("parallel",)),
        name="hy_spectrum",
    )(kf, kb, *tabs)


def _hy_conv_kernel(plan, hv_ref, c_ref, x0g_ref, bias_ref, fa_ref, f2r_ref, f2i_ref, twr_ref, twi_ref,
                    e1r_ref, e1i_ref, e2r_ref, e2i_ref, o_ref, pr_ref, pi_ref, y_ref):
    n2, kh, h1 = plan.n2, plan.kh, plan.h1
    x_ref = hv_ref.at[0]
    _dft_cols(plan, x_ref, fa_ref, pr_ref, pi_ref, _dot_fast)

    f2r, f2i = f2r_ref[...], f2i_ref[...]
    inv = jnp.concatenate([jnp.concatenate([f2r, f2i], axis=1), jnp.concatenate([-f2i, f2r], axis=1)],
                          axis=0).astype(BF16)

    def rows_body(k1, _):
        r0 = pl.multiple_of(k1 * n2, n2)
        xr, xi = _twiddled_row_dft(plan, k1, f2r, f2i, twr_ref, twi_ref,
                                   pr_ref[pl.ds(r0, n2), :], pi_ref[pl.ds(r0, n2), :], _dot_fast)
        cr, ci = c_ref[0, pl.ds(r0, n2), :], c_ref[1, pl.ds(r0, n2), :]
        z = jnp.concatenate([xr * cr - xi * ci, xr * ci + xi * cr], axis=0)
        b = _dot_fast(inv, z)
        pr_ref[pl.ds(r0, n2), :] = b[:n2]
        pi_ref[pl.ds(r0, n2), :] = b[n2:]
        return 0

    lax.fori_loop(0, plan.kv, rows_body, 0, unroll=2)

    e1r, e1i = e1r_ref[...], e1i_ref[...]

    def cols_body(j, _):
        e2r = e2r_ref[pl.ds(j, 1), :]
        e2i = e2i_ref[pl.ds(j, 1), :]
        gr = e1r * e2r - e1i * e2i
        gi = e1r * e2i + e1i * e2r
        g = jnp.concatenate([gr, -gi], axis=1)
        b = jnp.concatenate([pr_ref[pl.ds(j, kh, stride=n2), :], pi_ref[pl.ds(j, kh, stride=n2), :]], axis=0)
        y_ref[pl.ds(j, h1, stride=n2), :] = _dot_fast(g, b)
        return 0

    lax.fori_loop(0, n2, cols_body, 0, unroll=4)

    bias = bias_ref[...]
    chunk = min(plan.L, 512)

    def out_body(t, _):
        r0 = pl.multiple_of(t * chunk, chunk)
        hv = hv_ref[0, pl.ds(r0, chunk), :]
        y = y_ref[pl.ds(r0, chunk), :] + hv * bias
        o_ref[0, pl.ds(r0, chunk), :] = (y * x0g_ref[0, pl.ds(r0, chunk), :].astype(F32)).astype(o_ref.dtype)
        return 0

    lax.fori_loop(0, plan.L // chunk, out_body, 0)


def _hy_conv(plan, hv, spec, x0g, hy_bias):
    B, L, _ = hv.shape
    rows = plan.kh * plan.n2
    cb = LANES
    nc = HY_WIDTH // cb
    tabs = plan.fwd_tables() + plan.inv_tables()
    full = lambda a: pl.BlockSpec(a.shape, lambda c, b: (0,) * a.ndim)
    seq = pl.BlockSpec((1, L, cb), lambda c, b: (b, 0, c))
    return pl.pallas_call(
        functools.partial(_hy_conv_kernel, plan),
        grid=(nc, B),
        in_specs=[seq,
                  pl.BlockSpec((2, rows, cb), lambda c, b: (0, 0, c), pipeline_mode=pl.Buffered(1)),
                  seq,
                  pl.BlockSpec((1, cb), lambda c, b: (0, c))] + [full(t) for t in tabs],
        out_specs=seq,
        out_shape=jax.ShapeDtypeStruct((B, L, HY_WIDTH), BF16),
        scratch_shapes=[pltpu.VMEM((rows, cb), F32), pltpu.VMEM((rows, cb), F32),
                        pltpu.VMEM((L, cb), F32)],
        compiler_params=_cparams(("parallel", "parallel")),
        name="hy_conv",
    )(hv, spec, x0g, hy_bias.reshape(1, HY_WIDTH), *tabs)


def _flash_pipeline(n, score_fn, v_chunk_fn, m_ref, acc_ref, s_refs, mp_refs):
    u = 4 if n % 4 == 0 else 2
    assert n % u == 0
    m_ref[...] = jnp.full(m_ref.shape, -jnp.inf, F32)
    acc_ref[...] = jnp.zeros(acc_ref.shape, F32)

    def scores(j, slot):
        s = score_fn(j)
        s_refs[slot][...] = s
        mp_refs[slot][...] = jnp.max(s, axis=0, keepdims=True)

    def update(j, slot):
        m_old = m_ref[...]
        m_new = jnp.maximum(m_old, mp_refs[slot][...])
        alpha = jnp.exp2(m_old - m_new)
        p = jnp.exp2(s_refs[slot][...] - m_new).astype(BF16)
        acc_ref[...] = alpha * acc_ref[...] + _dotf(v_chunk_fn(j), p)
        m_ref[...] = m_new

    scores(0, 0)

    def body(jj, _):
        j = u * jj
        for d in range(u):
            scores(j + d + 1 if d + 1 < u else jnp.minimum(j + u, n - 1), (d + 1) % 2)
            update(j + d, d % 2)
        return 0

    lax.fori_loop(0, n // u, body, 0)


def _gqa_kernel(tk, q0_ref, q1_ref, q2_ref, q3_ref, k_ref, v_ref, sg_ref, o_ref,
                m_ref, acc_ref, s0_ref, s1_ref, mp0_ref, mp1_ref):
    qt = jnp.concatenate([q0_ref[0, 0], q1_ref[0, 0], q2_ref[0, 0], q3_ref[0, 0]], axis=1)
    tq = q0_ref.shape[-1]
    L = k_ref.shape[2]

    def score_fn(j):
        return _dotf(k_ref[0, 0, pl.ds(pl.multiple_of(j * tk, tk), tk), :], qt)

    def v_chunk_fn(j):
        return v_ref[0, 0, :, pl.ds(pl.multiple_of(j * tk, tk), tk)]

    _flash_pipeline(L // tk, score_fn, v_chunk_fn, m_ref, acc_ref, (s0_ref, s1_ref), (mp0_ref, mp1_ref))
    acc = acc_ref[...]
    o = acc[:HEAD_DIM] * (1.0 / acc[HEAD_DIM:HEAD_DIM + 1])
    ot = jnp.concatenate([o[:, h * tq:(h + 1) * tq].T for h in range(GQA_GROUP)], axis=1)
    o_ref[0] = (ot * sg_ref[0].astype(F32)).astype(o_ref.dtype)


def _gqa_attn(qT, k, vT, sg, tq, tk):
    B, _, _, L = qT.shape
    gw = GQA_GROUP * HEAD_DIM
    nq = GQA_GROUP * tq
    qspec = lambda hh: pl.BlockSpec((1, 1, HEAD_DIM, tq), lambda b, g, i: (b, GQA_GROUP * g + hh, 0, i))
    return pl.pallas_call(
        functools.partial(_gqa_kernel, tk),
        grid=(B, GQA_KV_HEADS, L // tq),
        in_specs=[qspec(0), qspec(1), qspec(2), qspec(3),
                  pl.BlockSpec((1, 1, L, HEAD_DIM), lambda b, g, i: (b, g, 0, 0)),
                  pl.BlockSpec((1, 1, HEAD_DIM + ONES_ROWS, L), lambda b, g, i: (b, g, 0, 0)),
                  pl.BlockSpec((1, tq, gw), lambda b, g, i: (b, i, g))],
        out_specs=pl.BlockSpec((1, tq, gw), lambda b, g, i: (b, i, g)),
        out_shape=jax.ShapeDtypeStruct((B, L, GQA_HEADS * HEAD_DIM), BF16),
        scratch_shapes=[pltpu.VMEM((1, nq), F32), pltpu.VMEM((HEAD_DIM + ONES_ROWS, nq), F32),
                        pltpu.VMEM((tk, nq), F32), pltpu.VMEM((tk, nq), F32),
                        pltpu.VMEM((1, nq), F32), pltpu.VMEM((1, nq), F32)],
        compiler_params=_cparams(("parallel", "parallel", "parallel")),
        name="gqa_attn",
    )(qT, qT, qT, qT, k, vT, sg)


def _t5_bias_kernel(t, rb_ref, o_ref):
    h = pl.program_id(0)
    kk = lax.broadcasted_iota(jnp.int32, (t, t), 0)
    qq = lax.broadcasted_iota(jnp.int32, (t, t), 1)
    nb = REL_BUCKETS // 2
    max_exact = nb // 2
    thresholds = [int(math.ceil(max_exact * (REL_MAX_DIST / max_exact) ** (j / (nb - max_exact)) - 1e-9))
                  for j in range(1, nb - max_exact)]
    for idx in range(5):
        rel = (idx - 2) * t + kk - qq
        n = jnp.abs(rel)
        large = jnp.full((t, t), max_exact, jnp.int32)
        for th in thresholds:
            large = large + (n >= th).astype(jnp.int32)
        bucket = jnp.where(rel > 0, nb, 0) + jnp.where(n < max_exact, n, large)
        val = jnp.zeros((t, t), F32)
        for b in range(REL_BUCKETS):
            val = jnp.where(bucket == b, rb_ref[h, b], val)
        o_ref[0, idx] = val * LOG2E


def _t5_bias_tiles(rel_bias, t):
    return pl.pallas_call(
        functools.partial(_t5_bias_kernel, t),
        grid=(DIFF_HEADS,),
        in_specs=[pl.BlockSpec(memory_space=pltpu.SMEM)],
        out_specs=pl.BlockSpec((1, 5, t, t), lambda h: (h, 0, 0, 0)),
        out_shape=jax.ShapeDtypeStruct((DIFF_HEADS, 5, t, t), F32),
        compiler_params=_cparams(("parallel",)),
        name="t5_bias_tiles",
    )(rel_bias.T)


def _diff_kernel(t, lam_init, q_ref, k0_ref, k1_ref, v_ref, bias_ref, lam_ref, g_ref, sg_ref, o_ref,
                 m_ref, acc_ref, s0_ref, s1_ref, mp0_ref, mp1_ref):
    i = pl.program_id(2)
    L = k0_ref.shape[2]
    hw = 2 * HEAD_DIM
    qt = q_ref[0, 0]

    def score_fn(j):
        r0 = pl.multiple_of(j * t, t)
        bias = bias_ref[0, jnp.clip(j - i, -2, 2) + 2]
        return jnp.concatenate(
            [_dotf(k0_ref[0, 0, pl.ds(r0, t), :], qt[:HEAD_DIM]) + bias,
             _dotf(k1_ref[0, 0, pl.ds(r0, t), :], qt[HEAD_DIM:]) + bias], axis=1)

    def v_chunk_fn(j):
        return v_ref[0, 0, :, pl.ds(pl.multiple_of(j * t, t), t)]

    _flash_pipeline(L // t, score_fn, v_chunk_fn, m_ref, acc_ref, (s0_ref, s1_ref), (mp0_ref, mp1_ref))

    lp = lam_ref[...]
    lam = (jnp.exp(jnp.sum(lp[0:1] * lp[1:2], axis=1, keepdims=True))
           - jnp.exp(jnp.sum(lp[2:3] * lp[3:4], axis=1, keepdims=True)) + lam_init)
    acc = acc_ref[...]
    o = acc[:hw] * (1.0 / acc[hw:hw + 1])
    o = o[:, :t] - lam * o[:, t:]
    ms = jnp.mean(o * o, axis=0, keepdims=True)
    o = o * lax.rsqrt(ms + EPS) * g_ref[...] * (1.0 - lam_init)
    o_ref[0] = (o.T * sg_ref[0].astype(F32)).astype(o_ref.dtype)


def _diff_attn(qT, k, vT, bias_tiles, lam_params, subln_g, sg, lam_init, t):
    B, _, _, L = qT.shape
    hw = 2 * HEAD_DIM
    g_tab = jnp.broadcast_to(subln_g[:, None], (hw, t))
    kspec = lambda c: pl.BlockSpec((1, 1, L, HEAD_DIM), lambda b, h, i: (b, 2 * h + c, 0, 0))
    return pl.pallas_call(
        functools.partial(_diff_kernel, t, lam_init),
        grid=(B, DIFF_HEADS, L // t),
        in_specs=[pl.BlockSpec((1, 1, hw, t), lambda b, h, i: (b, h, 0, i)),
                  kspec(0), kspec(1),
                  pl.BlockSpec((1, 1, hw + ONES_ROWS, L), lambda b, h, i: (b, h, 0, 0)),
                  pl.BlockSpec((1, 5, t, t), lambda b, h, i: (h, 0, 0, 0)),
                  pl.BlockSpec((4, HEAD_DIM), lambda b, h, i: (0, 0)),
                  pl.BlockSpec((hw, t), lambda b, h, i: (0, 0)),
                  pl.BlockSpec((1, t, hw), lambda b, h, i: (b, i, h))],
        out_specs=pl.BlockSpec((1, t, hw), lambda b, h, i: (b, i, h)),
        out_shape=jax.ShapeDtypeStruct((B, L, DIFF_HEADS * hw), BF16),
        scratch_shapes=[pltpu.VMEM((1, 2 * t), F32), pltpu.VMEM((hw + ONES_ROWS, 2 * t), F32),
                        pltpu.VMEM((t, 2 * t), F32), pltpu.VMEM((t, 2 * t), F32),
                        pltpu.VMEM((1, 2 * t), F32), pltpu.VMEM((1, 2 * t), F32)],
        compiler_params=_cparams(("parallel", "parallel", "parallel")),
        name="diff_attn",
    )(qT, k, k, vT, bias_tiles, lam_params, g_tab, sg)


def _merge_kernel(final, x_ref, ng_ref, yh_ref, yg_ref, yd_ref, wm_ref, bm_ref, wh_ref, wg_ref, wd_ref,
                  wo_ref, fg_ref, o_ref):
    x = x_ref[...]
    ms = jnp.mean(x * x, axis=-1, keepdims=True)
    h = (x * lax.rsqrt(ms + EPS) * ng_ref[...]).astype(BF16)
    merged = None
    for b, (y_ref, w_ref) in enumerate(((yh_ref, wh_ref), (yg_ref, wg_ref), (yd_ref, wd_ref))):
        z = _dotf(h, wm_ref[:, b * D_MODEL:(b + 1) * D_MODEL]) + bm_ref[:, b * D_MODEL:(b + 1) * D_MODEL]
        gate = 1.0 / (1.0 + jnp.exp(-z))
        term = gate * _dotf(y_ref[...], w_ref[...])
        merged = term if merged is None else merged + term
    y = x + _dotf(merged.astype(BF16), wo_ref[...])
    if final:
        ms = jnp.mean(y * y, axis=-1, keepdims=True)
        y = y * lax.rsqrt(ms + EPS) * fg_ref[...]
    o_ref[...] = y


def _merge_out(x2, norm_g, y_hy, y_gq, y_df, w_merge, b_merge, w_hy, w_gq, w_df, w_out, final_g, final, tm):
    T = x2.shape[0]
    full = lambda a: pl.BlockSpec(a.shape, lambda i: (0,) * a.ndim)
    tok = lambda w: pl.BlockSpec((tm, w), lambda i: (i, 0))
    ng = norm_g.reshape(1, D_MODEL)
    bm = b_merge.reshape(1, -1)
    fg = final_g.reshape(1, D_MODEL)
    return pl.pallas_call(
        functools.partial(_merge_kernel, final),
        grid=(T // tm,),
        in_specs=[tok(D_MODEL), full(ng), tok(HY_WIDTH), tok(GQA_HEADS * HEAD_DIM), tok(DIFF_HEADS * 2 * HEAD_DIM),
                  full(w_merge), full(bm), full(w_hy), full(w_gq), full(w_df), full(w_out), full(fg)],
        out_specs=tok(D_MODEL),
        out_shape=jax.ShapeDtypeStruct((T, D_MODEL), F32),
        compiler_params=_cparams(("parallel",)),
        name="merge_out",
    )(x2, ng, y_hy, y_gq, y_df, w_merge, bm, w_hy, w_gq, w_df, w_out, fg)


def _rope_tables_t(L):
    rows = L // GRID_W
    row = jnp.broadcast_to(jnp.arange(rows, dtype=F32)[:, None], (rows, GRID_W)).reshape(L)
    col = jnp.broadcast_to(jnp.arange(GRID_W, dtype=F32)[None, :], (rows, GRID_W)).reshape(L)
    n_freq = HEAD_DIM // 4
    inv_freq = ROPE_THETA ** (-jnp.arange(n_freq, dtype=F32) / n_freq)
    ang = jnp.concatenate([row[:, None] * inv_freq, col[:, None] * inv_freq], axis=-1)
    return jnp.cos(ang).T, jnp.sin(ang).T


def _hyena_positions(L):
    t01 = jnp.linspace(0.0, 1.0, L, dtype=F32)[:, None]
    bands = (FILT_EMB - 1) // 2
    w = 2.0 * math.pi * jnp.arange(L, dtype=F32)[:, None] / L
    f = jnp.linspace(1e-4, bands - 1, bands, dtype=F32)[None, :]
    z = jnp.concatenate([t01, jnp.cos(f * w), -jnp.sin(f * w)], axis=-1)
    z = jnp.pad(z, ((0, 0), (0, FILT_EMB_PAD - FILT_EMB)))
    max_decay = math.log(HY_TARGET) / HY_FAST_DECAY
    min_decay = math.log(HY_TARGET) / HY_SLOW_DECAY
    deltas = jnp.linspace(min_decay, max_decay, HY_WIDTH, dtype=F32)
    window = jnp.exp(-t01 * jnp.abs(deltas)[None, :]) + HY_SHIFT
    return z, window


def _tile(L, want):
    t = min(L, want)
    assert L % t == 0
    return t


def _trunk(x, p, bias_tiles, attn_t):
    B, L, _ = x.shape
    depth = p['w_in'].shape[0]
    cos_t, sin_t = _rope_tables_t(L)
    z_pad, window = _hyena_positions(L)
    plan = _FftPlan(L)
    tm = _tile(L, 256)
    x2 = x.reshape(B * L, D_MODEL)
    for l in range(depth):
        (hy_u, hy_sg, gq_qT, gq_k, gq_vT, gq_sg, df_qT, df_k, df_vT, df_sg) = _in_proj(
            x2.reshape(B, L, D_MODEL), p['norm_g'][l], p['w_in_bf'][l], p['q_norm_g'][l], p['k_norm_g'][l],
            cos_t, sin_t, tm)

        hv, x0g = _hy_pre(hy_u, p['hy_conv_w'][l], p['hy_conv_b'][l], hy_sg, _tile(L, 512))
        w1_pad = jnp.pad(p['hy_f_w1'][l], ((0, FILT_EMB_PAD - FILT_EMB), (0, 0)))
        kf, kb = _hy_filter(z_pad, window, w1_pad, p['hy_f_b1'][l], p['hy_f_w2'][l], p['hy_f_b2'][l],
                            p['hy_f_wout'][l], p['hy_f_freq'][l], _tile(L, 512))
        spec = _hy_spectrum(plan, kf, kb)
        y_hy = _hy_conv(plan, hv, spec, x0g, p['hy_bias'][l])

        y_gq = _gqa_attn(gq_qT, gq_k, gq_vT, gq_sg, _tile(L, 128), _tile(L // 2, 512))

        lam_init = 0.8 - 0.6 * math.exp(-0.3 * l)
        lam_params = jnp.stack([p['lam_q1'][l], p['lam_k1'][l], p['lam_q2'][l], p['lam_k2'][l]])
        y_df = _diff_attn(df_qT, df_k, df_vT, bias_tiles, lam_params, p['diff_subln_g'][l], df_sg, lam_init, attn_t)

        x2 = _merge_out(x2, p['norm_g'][l], y_hy.reshape(B * L, -1), y_gq.reshape(B * L, -1),
                        y_df.reshape(B * L, -1), p['w_merge_bf'][l], p['b_merge'][l], p['w_branch_hy_bf'][l],
                        p['w_branch_gqa_bf'][l], p['w_branch_diff_bf'][l], p['w_out_bf'][l], p['final_g'],
                        l == depth - 1, _tile(B * L, 512))
    return x2.reshape(B, L, D_MODEL)


def kernel(x_prompt, x_sample, rel_bias, norm_g, w_in, hy_conv_w, hy_conv_b, hy_f_w1, hy_f_b1, hy_f_w2, hy_f_b2, hy_f_wout, hy_f_freq, hy_bias, q_norm_g, k_norm_g, lam_q1, lam_k1, lam_q2, lam_k2, diff_subln_g, w_branch_hy, w_branch_gqa, w_branch_diff, w_merge, b_merge, w_out, final_g):
    p = dict(norm_g=norm_g, hy_conv_w=hy_conv_w, hy_conv_b=hy_conv_b, hy_f_w1=hy_f_w1,
             hy_f_b1=hy_f_b1, hy_f_w2=hy_f_w2, hy_f_b2=hy_f_b2, hy_f_wout=hy_f_wout, hy_f_freq=hy_f_freq,
             hy_bias=hy_bias, q_norm_g=q_norm_g, k_norm_g=k_norm_g, lam_q1=lam_q1, lam_k1=lam_k1,
             lam_q2=lam_q2, lam_k2=lam_k2, diff_subln_g=diff_subln_g, b_merge=b_merge, final_g=final_g,
             w_in=w_in,
             w_in_bf=w_in.astype(BF16), w_merge_bf=w_merge.astype(BF16), w_out_bf=w_out.astype(BF16),
             w_branch_hy_bf=w_branch_hy.astype(BF16), w_branch_gqa_bf=w_branch_gqa.astype(BF16),
             w_branch_diff_bf=w_branch_diff.astype(BF16))
    outs = []
    for x in (x_prompt, x_sample):
        attn_t = _tile(x.shape[1], 256)
        bias_tiles = _t5_bias_tiles(rel_bias, attn_t)
        outs.append(_trunk(x, p, bias_tiles, attn_t))
    return tuple(outs)
```

```python
import functools
import math

import numpy as np
import jax
import jax.numpy as jnp
from jax import lax
from jax.experimental import pallas as pl
from jax.experimental.pallas import tpu as pltpu

D_MODEL = 1024
HEAD_DIM = 64
EPS = 1e-6
GRID_W = 64
ROPE_THETA = 10000.0

HY_WIDTH = 512
FILT_EMB = 33
FILT_EMB_PAD = 64
FILT_ORDER = 64
FILT_INNER = 2
HY_FAST_DECAY = 0.3
HY_SLOW_DECAY = 1.5
HY_TARGET = 1e-2
HY_SHIFT = 0.0

GQA_HEADS = 8
GQA_KV_HEADS = 2
GQA_GROUP = GQA_HEADS // GQA_KV_HEADS
DIFF_HEADS = 4
REL_BUCKETS = 32
REL_MAX_DIST = 128

C_HY_U = 0
C_HY_G = 1536
C_GQ_Q = 2048
C_GQ_K = 2560
C_GQ_V = 2688
C_GQ_G = 2816
C_DF_Q = 3328
C_DF_K = 3840
C_DF_V = 4352
C_DF_G = 4864
IN_COLS = 5376

LANES = 128
ONES_ROWS = 16
VMEM_LIMIT = 56 * 1024 * 1024
LOG2E = math.log2(math.e)
HI = lax.Precision.HIGHEST
F32 = jnp.float32
BF16 = jnp.bfloat16


def _cparams(sem):
    return pltpu.CompilerParams(dimension_semantics=sem, vmem_limit_bytes=VMEM_LIMIT)


def _silu(x):
    return x * (1.0 / (1.0 + jnp.exp(-x)))


def _dotf(a, b):
    return jnp.dot(a, b, preferred_element_type=F32)


def _norm_rope_t(xt, g_tab, cos, sin):
    ms = jnp.mean(xt * xt, axis=1, keepdims=True)
    xn = xt * lax.rsqrt(ms + EPS) * g_tab[None]
    half = HEAD_DIM // 2
    x1, x2 = xn[:, :half, :], xn[:, half:, :]
    c, s = cos[None], sin[None]
    return jnp.concatenate([x1 * c - x2 * s, x2 * c + x1 * s], axis=1)


def _in_proj_kernel(x_ref, ng_ref, w_ref, qg_ref, kg_ref, cos_ref, sin_ref,
                    hyu_ref, hysg_ref, gqq_ref, gqk_ref, gqv_ref, gqsg_ref,
                    dfq_ref, dfk_ref, dfv_ref, dfsg_ref):
    x = x_ref[0]
    tm = x.shape[0]
    ms = jnp.mean(x * x, axis=-1, keepdims=True)
    h = (x * lax.rsqrt(ms + EPS) * ng_ref[...]).astype(BF16)

    def proj(lo, hi):
        return _dotf(h, w_ref[:, lo:hi])

    hyu_ref[0] = proj(C_HY_U, C_HY_G)
    hysg_ref[0] = _silu(proj(C_HY_G, C_GQ_Q)).astype(BF16)

    cos, sin = cos_ref[...], sin_ref[...]
    scale = HEAD_DIM ** -0.5
    qt = proj(C_GQ_Q, C_GQ_K).T.reshape(GQA_HEADS, HEAD_DIM, tm)
    gqq_ref[0] = (_norm_rope_t(qt, qg_ref[...], cos, sin) * (scale * LOG2E)).astype(BF16)
    kt = proj(C_GQ_K, C_GQ_V).T.reshape(GQA_KV_HEADS, HEAD_DIM, tm)
    kt = _norm_rope_t(kt, kg_ref[...], cos, sin)
    for g in range(GQA_KV_HEADS):
        gqk_ref[0, g] = kt[g].T.astype(BF16)
    ones_rows = (lax.broadcasted_iota(jnp.int32, (ONES_ROWS, tm), 0) == 0).astype(BF16)
    vt = proj(C_GQ_V, C_GQ_G).T.astype(BF16)
    for g in range(GQA_KV_HEADS):
        gqv_ref[0, g, :HEAD_DIM, :] = vt[g * HEAD_DIM:(g + 1) * HEAD_DIM]
        gqv_ref[0, g, HEAD_DIM:, :] = ones_rows
    gqsg_ref[0] = _silu(proj(C_GQ_G, C_DF_Q)).astype(BF16)

    dfq_ref[0] = (proj(C_DF_Q, C_DF_K) * (scale * LOG2E)).T.reshape(DIFF_HEADS, 2 * HEAD_DIM, tm).astype(BF16)
    dk = proj(C_DF_K, C_DF_V).astype(BF16)
    for j in range(2 * DIFF_HEADS):
        dfk_ref[0, j] = dk[:, j * HEAD_DIM:(j + 1) * HEAD_DIM]
    dvt = proj(C_DF_V, C_DF_G).T.astype(BF16)
    for hh in range(DIFF_HEADS):
        dfv_ref[0, hh, :2 * HEAD_DIM, :] = dvt[hh * 2 * HEAD_DIM:(hh + 1) * 2 * HEAD_DIM]
        dfv_ref[0, hh, 2 * HEAD_DIM:, :] = ones_rows
    dfsg_ref[0] = _silu(proj(C_DF_G, IN_COLS)).astype(BF16)


def _in_proj(x, norm_g, w_in_bf, q_g, k_g, cos_t, sin_t, tm):
    B, L, _ = x.shape
    nt = L // tm
    qg_tab = jnp.broadcast_to(q_g[:, None], (HEAD_DIM, tm))
    kg_tab = jnp.broadcast_to(k_g[:, None], (HEAD_DIM, tm))
    full = lambda shape: pl.BlockSpec(shape, lambda b, i: (0,) * len(shape))
    out_shapes = (
        jax.ShapeDtypeStruct((B, L, 3 * HY_WIDTH), F32),
        jax.ShapeDtypeStruct((B, L, HY_WIDTH), BF16),
        jax.ShapeDtypeStruct((B, GQA_HEADS, HEAD_DIM, L), BF16),
        jax.ShapeDtypeStruct((B, GQA_KV_HEADS, L, HEAD_DIM), BF16),
        jax.ShapeDtypeStruct((B, GQA_KV_HEADS, HEAD_DIM + ONES_ROWS, L), BF16),
        jax.ShapeDtypeStruct((B, L, GQA_HEADS * HEAD_DIM), BF16),
        jax.ShapeDtypeStruct((B, DIFF_HEADS, 2 * HEAD_DIM, L), BF16),
        jax.ShapeDtypeStruct((B, 2 * DIFF_HEADS, L, HEAD_DIM), BF16),
        jax.ShapeDtypeStruct((B, DIFF_HEADS, 2 * HEAD_DIM + ONES_ROWS, L), BF16),
        jax.ShapeDtypeStruct((B, L, DIFF_HEADS * 2 * HEAD_DIM), BF16),
    )
    tok = lambda w: pl.BlockSpec((1, tm, w), lambda b, i: (b, i, 0))
    tr = lambda h, d: pl.BlockSpec((1, h, d, tm), lambda b, i: (b, 0, 0, i))
    rows = lambda h: pl.BlockSpec((1, h, tm, HEAD_DIM), lambda b, i: (b, 0, i, 0))
    return pl.pallas_call(
        _in_proj_kernel,
        grid=(B, nt),
        in_specs=[
            tok(D_MODEL),
            full((1, D_MODEL)),
            full((D_MODEL, IN_COLS)),
            full((HEAD_DIM, tm)),
            full((HEAD_DIM, tm)),
            pl.BlockSpec((HEAD_DIM // 2, tm), lambda b, i: (0, i)),
            pl.BlockSpec((HEAD_DIM // 2, tm), lambda b, i: (0, i)),
        ],
        out_specs=(
            tok(3 * HY_WIDTH), tok(HY_WIDTH),
            tr(GQA_HEADS, HEAD_DIM), rows(GQA_KV_HEADS), tr(GQA_KV_HEADS, HEAD_DIM + ONES_ROWS),
            tok(GQA_HEADS * HEAD_DIM),
            tr(DIFF_HEADS, 2 * HEAD_DIM), rows(2 * DIFF_HEADS), tr(DIFF_HEADS, 2 * HEAD_DIM + ONES_ROWS),
            tok(DIFF_HEADS * 2 * HEAD_DIM),
        ),
        out_shape=out_shapes,
        compiler_params=_cparams(("parallel", "parallel")),
        name="in_proj",
    )(x, norm_g.reshape(1, D_MODEL), w_in_bf, qg_tab, kg_tab, cos_t, sin_t)


def _hy_pre_kernel(u_ref, prev_ref, next_ref, w_ref, b_ref, sg_ref, hv_ref, x0g_ref):
    i = pl.program_id(1)
    nt = pl.num_programs(1)
    u = u_ref[0]
    tl = u.shape[0]
    prev_row = jnp.where(i > 0, prev_ref[0, 0, 7:8, :], 0.0)
    next_row = jnp.where(i < nt - 1, next_ref[0, 0, 0:1, :], 0.0)
    row = lax.broadcasted_iota(jnp.int32, u.shape, 0)
    up = jnp.where(row == 0, prev_row, pltpu.roll(u, 1, 0))
    dn = jnp.where(row == tl - 1, next_row, pltpu.roll(u, tl - 1, 0))
    w = w_ref[...]
    hy = up * w[0:1] + u * w[1:2] + dn * w[2:3] + b_ref[...]
    x0 = hy[:, :HY_WIDTH]
    x1 = hy[:, HY_WIDTH:2 * HY_WIDTH]
    hv = hy[:, 2 * HY_WIDTH:]
    hv_ref[0] = hv * x1
    x0g_ref[0] = (x0 * sg_ref[0].astype(F32)).astype(x0g_ref.dtype)


def _hy_pre(hy_u, conv_w, conv_b, hy_sg, tl):
    B, L, W3 = hy_u.shape
    nt = L // tl
    g8 = tl // 8
    u4 = hy_u.reshape(B, L // 8, 8, W3)
    return pl.pallas_call(
        _hy_pre_kernel,
        grid=(B, nt),
        in_specs=[
            pl.BlockSpec((1, tl, W3), lambda b, i: (b, i, 0)),
            pl.BlockSpec((1, 1, 8, W3), lambda b, i: (b, jnp.maximum(i * g8 - 1, 0), 0, 0)),
            pl.BlockSpec((1, 1, 8, W3), lambda b, i: (b, jnp.minimum((i + 1) * g8, L // 8 - 1), 0, 0)),
            pl.BlockSpec((3, W3), lambda b, i: (0, 0)),
            pl.BlockSpec((1, W3), lambda b, i: (0, 0)),
            pl.BlockSpec((1, tl, HY_WIDTH), lambda b, i: (b, i, 0)),
        ],
        out_specs=(
            pl.BlockSpec((1, tl, HY_WIDTH), lambda b, i: (b, i, 0)),
            pl.BlockSpec((1, tl, HY_WIDTH), lambda b, i: (b, i, 0)),
        ),
        out_shape=(
            jax.ShapeDtypeStruct((B, L, HY_WIDTH), F32),
            jax.ShapeDtypeStruct((B, L, HY_WIDTH), BF16),
        ),
        compiler_params=_cparams(("parallel", "parallel")),
        name="hy_pre",
    )(hy_u, u4, u4, conv_w, conv_b.reshape(1, W3), hy_sg)


def _hy_filter_kernel(z_ref, win_ref, w1_ref, b1_ref, w2_ref, b2_ref, wout_ref, freq_ref, kf_ref, kb_ref):
    i = pl.program_id(0)
    freq = freq_ref[...]
    a = jnp.sin(freq * (jnp.dot(z_ref[...], w1_ref[...], precision=HI, preferred_element_type=F32) + b1_ref[...]))
    for j in range(FILT_INNER):
        a = jnp.sin(freq * (jnp.dot(a, w2_ref[j], precision=HI, preferred_element_type=F32) + b2_ref[j:j + 1, :]))
    hf = jnp.dot(a, wout_ref[...], precision=HI, preferred_element_type=F32)
    win = win_ref[...]
    kf_ref[...] = hf[:, :HY_WIDTH] * win
    row = lax.broadcasted_iota(jnp.int32, win.shape, 0)
    kb_ref[...] = jnp.where((row == 0) & (i == 0), 0.0, hf[:, HY_WIDTH:] * win)


def _hy_filter(z_pad, window, w1_pad, b1, w2, b2, wout, freq, tl):
    L = z_pad.shape[0]
    full = lambda shape: pl.BlockSpec(shape, lambda i: (0,) * len(shape))
    return pl.pallas_call(
        _hy_filter_kernel,
        grid=(L // tl,),
        in_specs=[
            pl.BlockSpec((tl, FILT_EMB_PAD), lambda i: (i, 0)),
            pl.BlockSpec((tl, HY_WIDTH), lambda i: (i, 0)),
            full((FILT_EMB_PAD, FILT_ORDER)),
            full((1, FILT_ORDER)),
            full((FILT_INNER, FILT_ORDER, FILT_ORDER)),
            full((FILT_INNER, FILT_ORDER)),
            full((FILT_ORDER, 2 * HY_WIDTH)),
            full((1, FILT_ORDER)),
        ],
        out_specs=(
            pl.BlockSpec((tl, HY_WIDTH), lambda i: (i, 0)),
            pl.BlockSpec((tl, HY_WIDTH), lambda i: (i, 0)),
        ),
        out_shape=(
            jax.ShapeDtypeStruct((L, HY_WIDTH), F32),
            jax.ShapeDtypeStruct((L, HY_WIDTH), F32),
        ),
        compiler_params=_cparams(("parallel",)),
        name="hy_filter",
    )(z_pad, window, w1_pad, b1.reshape(1, -1), w2, b2, wout, freq.reshape(1, -1))


class _FftPlan:
    def __init__(self, L):
        n = 2 * L
        e = int(round(math.log2(n)))
        assert 2 ** e == n and e >= 8
        self.L, self.n = L, n
        self.n1 = 2 ** (e // 2)
        self.n2 = n // self.n1
        self.h1 = self.n1 // 2
        self.kh = self.h1 + 8
        self.kv = self.h1 + 1
        n1, n2, h1, kh = self.n1, self.n2, self.h1, self.kh
        k1 = np.arange(kh)[:, None]
        a = 2 * np.pi * k1 * np.arange(h1)[None, :] / n1
        self.fa = np.concatenate([np.cos(a), -np.sin(a)], axis=0).astype(np.float32)
        a = 2 * np.pi * np.arange(n2)[:, None] * np.arange(n2)[None, :] / n2
        self.f2r, self.f2i = np.cos(a).astype(np.float32), (-np.sin(a)).astype(np.float32)
        a = 2 * np.pi * k1 * np.arange(n2)[None, :] / n
        self.twr, self.twi = np.cos(a).astype(np.float32), (-np.sin(a)).astype(np.float32)
        wgt = np.where(np.arange(kh) <= h1, 2.0, 0.0)
        wgt[0] = 1.0
        wgt[h1] = 1.0
        a = 2 * np.pi * np.arange(h1)[:, None] * np.arange(kh)[None, :] / n1
        self.e1r = (np.cos(a) * wgt[None, :] / n).astype(np.float32)
        self.e1i = (np.sin(a) * wgt[None, :] / n).astype(np.float32)
        a = 2 * np.pi * np.arange(n2)[:, None] * np.arange(kh)[None, :] / n
        self.e2r, self.e2i = np.cos(a).astype(np.float32), np.sin(a).astype(np.float32)

    def fwd_tables(self):
        return [jnp.asarray(t) for t in (self.fa, self.f2r, self.f2i, self.twr, self.twi)]

    def inv_tables(self):
        return [jnp.asarray(t) for t in (self.e1r, self.e1i, self.e2r, self.e2i)]


def _dot_exact(a, b):
    return jnp.dot(a, b, precision=HI, preferred_element_type=F32)


def _dot_fast(a, b):
    return _dotf(a.astype(BF16), b.astype(BF16))


def _dft_cols(plan, x_ref, fa_ref, pr_ref, pi_ref, dot):
    fa = fa_ref[...]

    def body(n2, _):
        x = x_ref[pl.ds(n2, plan.h1, stride=plan.n2), :]
        y = dot(fa, x)
        pr_ref[pl.ds(n2, plan.kh, stride=plan.n2), :] = y[:plan.kh]
        pi_ref[pl.ds(n2, plan.kh, stride=plan.n2), :] = y[plan.kh:]
        return 0

    lax.fori_loop(0, plan.n2, body, 0, unroll=4)


def _twiddled_row_dft(plan, k1, f2r, f2i, twr_ref, twi_ref, ar, ai, dot):
    twr = twr_ref[pl.ds(k1, 1), :]
    twi = twi_ref[pl.ds(k1, 1), :]
    mr = f2r * twr - f2i * twi
    mi = f2r * twi + f2i * twr
    m = jnp.concatenate([jnp.concatenate([mr, -mi], axis=1), jnp.concatenate([mi, mr], axis=1)], axis=0)
    y = dot(m, jnp.concatenate([ar, ai], axis=0))
    return y[:plan.n2], y[plan.n2:]


def _hy_spectrum_kernel(plan, kf_ref, kb_ref, fa_ref, f2r_ref, f2i_ref, twr_ref, twi_ref, c_ref, pr_ref, pi_ref):
    f2r, f2i = f2r_ref[...], f2i_ref[...]
    n2 = plan.n2
    for which, src in enumerate((kf_ref, kb_ref)):
        _dft_cols(plan, src, fa_ref, pr_ref, pi_ref, _dot_exact)

        def body(k1, _):
            r0 = pl.multiple_of(k1 * n2, n2)
            xr, xi = _twiddled_row_dft(plan, k1, f2r, f2i, twr_ref, twi_ref,
                                       pr_ref[pl.ds(r0, n2), :], pi_ref[pl.ds(r0, n2), :], _dot_exact)
            if which == 0:
                c_ref[0, pl.ds(r0, n2), :] = xr
                c_ref[1, pl.ds(r0, n2), :] = xi
            else:
                c_ref[0, pl.ds(r0, n2), :] = c_ref[0, pl.ds(r0, n2), :] + xr
                c_ref[1, pl.ds(r0, n2), :] = c_ref[1, pl.ds(r0, n2), :] - xi
            return 0

        lax.fori_loop(0, plan.kh, body, 0, unroll=2)


def _hy_spectrum(plan, kf, kb):
    rows = plan.kh * plan.n2
    nc = HY_WIDTH // LANES
    tabs = plan.fwd_tables()
    full = lambda a: pl.BlockSpec(a.shape, lambda c: (0,) * a.ndim)
    return pl.pallas_call(
        functools.partial(_hy_spectrum_kernel, plan),
        grid=(nc,),
        in_specs=[pl.BlockSpec((plan.L, LANES), lambda c: (0, c)),
                  pl.BlockSpec((plan.L, LANES), lambda c: (0, c))] + [full(t) for t in tabs],
        out_specs=pl.BlockSpec((2, rows, LANES), lambda c: (0, 0, c)),
        out_shape=jax.ShapeDtypeStruct((2, rows, HY_WIDTH), F32),
        scratch_shapes=[pltpu.VMEM((rows, LANES), F32), pltpu.VMEM((rows, LANES), F32)],
        compiler_params=_cparams(("parallel",)),
        name="hy_spectrum",
    )(kf, kb, *tabs)


def _hy_conv_kernel(plan, hv_ref, c_ref, x0g_ref, bias_ref, fa_ref, f2r_ref, f2i_ref, twr_ref, twi_ref,
                    e1r_ref, e1i_ref, e2r_ref, e2i_ref, o_ref, pr_ref, pi_ref, y_ref):
    n2, kh, h1 = plan.n2, plan.kh, plan.h1
    x_ref = hv_ref.at[0]
    _dft_cols(plan, x_ref, fa_ref, pr_ref, pi_ref, _dot_fast)

    f2r, f2i = f2r_ref[...], f2i_ref[...]
    inv = jnp.concatenate([jnp.concatenate([f2r, f2i], axis=1), jnp.concatenate([-f2i, f2r], axis=1)],
                          axis=0).astype(BF16)

    def rows_body(k1, _):
        r0 = pl.multiple_of(k1 * n2, n2)
        xr, xi = _twiddled_row_dft(plan, k1, f2r, f2i, twr_ref, twi_ref,
                                   pr_ref[pl.ds(r0, n2), :], pi_ref[pl.ds(r0, n2), :], _dot_fast)
        cr, ci = c_ref[0, pl.ds(r0, n2), :], c_ref[1, pl.ds(r0, n2), :]
        z = jnp.concatenate([xr * cr - xi * ci, xr * ci + xi * cr], axis=0)
        b = _dot_fast(inv, z)
        pr_ref[pl.ds(r0, n2), :] = b[:n2]
        pi_ref[pl.ds(r0, n2), :] = b[n2:]
        return 0

    lax.fori_loop(0, plan.kv, rows_body, 0, unroll=2)

    e1r, e1i = e1r_ref[...], e1i_ref[...]

    def cols_body(j, _):
        e2r = e2r_ref[pl.ds(j, 1), :]
        e2i = e2i_ref[pl.ds(j, 1), :]
        gr = e1r * e2r - e1i * e2i
        gi = e1r * e2i + e1i * e2r
        g = jnp.concatenate([gr, -gi], axis=1)
        b = jnp.concatenate([pr_ref[pl.ds(j, kh, stride=n2), :], pi_ref[pl.ds(j, kh, stride=n2), :]], axis=0)
        y_ref[pl.ds(j, h1, stride=n2), :] = _dot_fast(g, b)
        return 0

    lax.fori_loop(0, n2, cols_body, 0, unroll=4)

    bias = bias_ref[...]
    chunk = min(plan.L, 512)

    def out_body(t, _):
        r0 = pl.multiple_of(t * chunk, chunk)
        hv = hv_ref[0, pl.ds(r0, chunk), :]
        y = y_ref[pl.ds(r0, chunk), :] + hv * bias
        o_ref[0, pl.ds(r0, chunk), :] = (y * x0g_ref[0, pl.ds(r0, chunk), :].astype(F32)).astype(o_ref.dtype)
        return 0

    lax.fori_loop(0, plan.L // chunk, out_body, 0)


def _hy_conv(plan, hv, spec, x0g, hy_bias):
    B, L, _ = hv.shape
    rows = plan.kh * plan.n2
    cb = LANES
    nc = HY_WIDTH // cb
    tabs = plan.fwd_tables() + plan.inv_tables()
    full = lambda a: pl.BlockSpec(a.shape, lambda c, b: (0,) * a.ndim)
    seq = pl.BlockSpec((1, L, cb), lambda c, b: (b, 0, c))
    return pl.pallas_call(
        functools.partial(_hy_conv_kernel, plan),
        grid=(nc, B),
        in_specs=[seq,
                  pl.BlockSpec((2, rows, cb), lambda c, b: (0, 0, c), pipeline_mode=pl.Buffered(1)),
                  seq,
                  pl.BlockSpec((1, cb), lambda c, b: (0, c))] + [full(t) for t in tabs],
        out_specs=seq,
        out_shape=jax.ShapeDtypeStruct((B, L, HY_WIDTH), BF16),
        scratch_shapes=[pltpu.VMEM((rows, cb), F32), pltpu.VMEM((rows, cb), F32),
                        pltpu.VMEM((L, cb), F32)],
        compiler_params=_cparams(("parallel", "parallel")),
        name="hy_conv",
    )(hv, spec, x0g, hy_bias.reshape(1, HY_WIDTH), *tabs)


def _chunks_per_trip(n):
    for u in (8, 4, 2):
        if n % u == 0:
            return u
    raise ValueError(f"need an even number of key chunks, got {n}")


LAG_JUMP_LIMIT = 32.0


def _flash_pipeline(n, score_fn, v_chunk_fn, m_ref, acc_ref, s_refs, mp_refs):
    u = _chunks_per_trip(n)
    m_ref[...] = jnp.full(m_ref.shape, -jnp.inf, F32)
    acc_ref[...] = jnp.zeros(acc_ref.shape, F32)

    def scores(j, slot):
        s = score_fn(j)
        s_refs[slot][...] = s
        mp_refs[slot][...] = jnp.max(s, axis=0, keepdims=True)

    def update(j, slot):
        m_old = m_ref[...]
        m_new = jnp.maximum(m_old, mp_refs[slot][...])
        alpha = jnp.exp2(m_old - m_new)
        p = jnp.exp2(s_refs[slot][...] - m_new).astype(BF16)
        acc_ref[...] = alpha * acc_ref[...] + _dotf(v_chunk_fn(j), p)
        m_ref[...] = m_new

    scores(0, 0)

    def body(jj, _):
        j = u * jj
        for d in range(u):
            scores(j + d + 1 if d + 1 < u else jnp.minimum(j + u, n - 1), (d + 1) % 2)
            update(j + d, d % 2)
        return 0

    lax.fori_loop(0, n // u, body, 0)


def _flash_lagged(n, score_fn, v_chunk_fn, m_ref, acc_ref, jump_ref, s_ref):
    u = _chunks_per_trip(n)
    acc_ref[...] = jnp.zeros(acc_ref.shape, F32)
    s0 = score_fn(0)
    s_ref[...] = s0
    m_ref[...] = jnp.max(s0, axis=0, keepdims=True)
    jump_ref[...] = jnp.zeros(jump_ref.shape, F32)

    def step(j, s):
        m_used = m_ref[...]
        mp = jnp.max(s, axis=0, keepdims=True)
        p = jnp.exp2(s - m_used).astype(BF16)
        m_next = jnp.maximum(m_used, mp)
        alpha = jnp.exp2(m_used - m_next)
        acc_ref[...] = (acc_ref[...] + _dotf(v_chunk_fn(j), p)) * alpha
        jump_ref[...] = jnp.maximum(jump_ref[...], mp - m_used)
        m_ref[...] = m_next

    def body(jj, _):
        j0 = u * jj
        s_cur = s_ref[...]
        for d in range(u):
            s_next = score_fn(j0 + d + 1 if d + 1 < u else jnp.minimum(j0 + u, n - 1))
            step(j0 + d, s_cur)
            s_cur = s_next
        s_ref[...] = s_cur
        return 0

    lax.fori_loop(0, n // u, body, 0)


def _run_flash(lagged, n, score_fn, v_chunk_fn, scratch, jump_ref):
    if lagged:
        m_ref, acc_ref, s_ref = scratch
        _flash_lagged(n, score_fn, v_chunk_fn, m_ref, acc_ref, jump_ref, s_ref)
    else:
        m_ref, acc_ref, s0_ref, s1_ref, mp0_ref, mp1_ref = scratch
        _flash_pipeline(n, score_fn, v_chunk_fn, m_ref, acc_ref, (s0_ref, s1_ref), (mp0_ref, mp1_ref))
    return acc_ref[...]


def _flash_scratch(lagged, rows, tk, nq):
    shapes = [pltpu.VMEM((1, nq), F32), pltpu.VMEM((rows + ONES_ROWS, nq), F32), pltpu.VMEM((tk, nq), F32)]
    if not lagged:
        shapes += [pltpu.VMEM((tk, nq), F32), pltpu.VMEM((1, nq), F32), pltpu.VMEM((1, nq), F32)]
    return shapes


def _with_exact_fallback(run):
    *outs, jump = run(True)
    outs = tuple(outs)
    return lax.cond(jnp.max(jump) > LAG_JUMP_LIMIT, lambda: tuple(run(False)), lambda: outs)


def _gqa_kernel(lagged, tk, q0_ref, q1_ref, q2_ref, q3_ref, k_ref, v_ref, sg_ref, o_ref, *rest):
    jump_ref, scratch = (rest[0].at[0], rest[1:]) if lagged else (None, rest)
    qt = jnp.concatenate([q0_ref[0, 0], q1_ref[0, 0], q2_ref[0, 0], q3_ref[0, 0]], axis=1)
    tq = q0_ref.shape[-1]
    L = k_ref.shape[2]

    def score_fn(j):
        return _dotf(k_ref[0, 0, pl.ds(pl.multiple_of(j * tk, tk), tk), :], qt)

    def v_chunk_fn(j):
        return v_ref[0, 0, :, pl.ds(pl.multiple_of(j * tk, tk), tk)]

    acc = _run_flash(lagged, L // tk, score_fn, v_chunk_fn, scratch, jump_ref)
    o = acc[:HEAD_DIM] * (1.0 / acc[HEAD_DIM:HEAD_DIM + 1])
    ot = jnp.concatenate([o[:, h * tq:(h + 1) * tq].T for h in range(GQA_GROUP)], axis=1)
    o_ref[0] = (ot * sg_ref[0].astype(F32)).astype(o_ref.dtype)


def _gqa_attn(qT, k, vT, sg, tq, tk):
    B, _, _, L = qT.shape
    gw = GQA_GROUP * HEAD_DIM
    nq = GQA_GROUP * tq
    nt = L // tq
    qspec = lambda hh: pl.BlockSpec((1, 1, HEAD_DIM, tq), lambda b, g, i: (b, GQA_GROUP * g + hh, 0, i))
    y_spec = pl.BlockSpec((1, tq, gw), lambda b, g, i: (b, i, g))
    y_shape = jax.ShapeDtypeStruct((B, L, GQA_HEADS * HEAD_DIM), BF16)
    jump_spec = pl.BlockSpec((1, 1, nq), lambda b, g, i: ((b * GQA_KV_HEADS + g) * nt + i, 0, 0))
    jump_shape = jax.ShapeDtypeStruct((B * GQA_KV_HEADS * nt, 1, nq), F32)

    def run(lagged):
        out = pl.pallas_call(
            functools.partial(_gqa_kernel, lagged, tk),
            grid=(B, GQA_KV_HEADS, nt),
            in_specs=[qspec(0), qspec(1), qspec(2), qspec(3),
                      pl.BlockSpec((1, 1, L, HEAD_DIM), lambda b, g, i: (b, g, 0, 0)),
                      pl.BlockSpec((1, 1, HEAD_DIM + ONES_ROWS, L), lambda b, g, i: (b, g, 0, 0)),
                      y_spec],
            out_specs=(y_spec, jump_spec) if lagged else y_spec,
            out_shape=(y_shape, jump_shape) if lagged else y_shape,
            scratch_shapes=_flash_scratch(lagged, HEAD_DIM, tk, nq),
            compiler_params=_cparams(("parallel", "parallel", "parallel")),
            name="gqa_attn" if lagged else "gqa_attn_exact",
        )(qT, qT, qT, qT, k, vT, sg)
        return out if lagged else (out,)

    return _with_exact_fallback(run)[0]


def _t5_bias_kernel(t, rb_ref, o_ref):
    h = pl.program_id(0)
    kk = lax.broadcasted_iota(jnp.int32, (t, t), 0)
    qq = lax.broadcasted_iota(jnp.int32, (t, t), 1)
    nb = REL_BUCKETS // 2
    max_exact = nb // 2
    thresholds = [int(math.ceil(max_exact * (REL_MAX_DIST / max_exact) ** (j / (nb - max_exact)) - 1e-9))
                  for j in range(1, nb - max_exact)]
    for idx in range(5):
        rel = (idx - 2) * t + kk - qq
        n = jnp.abs(rel)
        large = jnp.full((t, t), max_exact, jnp.int32)
        for th in thresholds:
            large = large + (n >= th).astype(jnp.int32)
        bucket = jnp.where(rel > 0, nb, 0) + jnp.where(n < max_exact, n, large)
        val = jnp.zeros((t, t), F32)
        for b in range(REL_BUCKETS):
            val = jnp.where(bucket == b, rb_ref[h, b], val)
        o_ref[0, idx] = val * LOG2E


def _t5_bias_tiles(rel_bias, t):
    return pl.pallas_call(
        functools.partial(_t5_bias_kernel, t),
        grid=(DIFF_HEADS,),
        in_specs=[pl.BlockSpec(memory_space=pltpu.SMEM)],
        out_specs=pl.BlockSpec((1, 5, t, t), lambda h: (h, 0, 0, 0)),
        out_shape=jax.ShapeDtypeStruct((DIFF_HEADS, 5, t, t), F32),
        compiler_params=_cparams(("parallel",)),
        name="t5_bias_tiles",
    )(rel_bias.T)


def _diff_kernel(lagged, t, tk, lam_init, q_ref, k0_ref, k1_ref, v_ref, bias_ref, lam_ref, g_ref, sg_ref,
                 o_ref, *rest):
    jump_ref, scratch = (rest[0].at[0], rest[1:]) if lagged else (None, rest)
    i = pl.program_id(2)
    L = k0_ref.shape[2]
    hw = 2 * HEAD_DIM
    tiles = tk // t
    qt = q_ref[0, 0]

    def score_fn(j):
        r0 = pl.multiple_of(j * tk, tk)
        bias = jnp.concatenate([bias_ref[0, jnp.clip(j * tiles + r - i, -2, 2) + 2] for r in range(tiles)],
                               axis=0)
        return jnp.concatenate(
            [_dotf(k0_ref[0, 0, pl.ds(r0, tk), :], qt[:HEAD_DIM]) + bias,
             _dotf(k1_ref[0, 0, pl.ds(r0, tk), :], qt[HEAD_DIM:]) + bias], axis=1)

    def v_chunk_fn(j):
        return v_ref[0, 0, :, pl.ds(pl.multiple_of(j * tk, tk), tk)]

    acc = _run_flash(lagged, L // tk, score_fn, v_chunk_fn, scratch, jump_ref)

    lp = lam_ref[...]
    lam = (jnp.exp(jnp.sum(lp[0:1] * lp[1:2], axis=1, keepdims=True))
           - jnp.exp(jnp.sum(lp[2:3] * lp[3:4], axis=1, keepdims=True)) + lam_init)
    o = acc[:hw] * (1.0 / acc[hw:hw + 1])
    o = o[:, :t] - lam * o[:, t:]
    ms = jnp.mean(o * o, axis=0, keepdims=True)
    o = o * lax.rsqrt(ms + EPS) * g_ref[...] * (1.0 - lam_init)
    o_ref[0] = (o.T * sg_ref[0].astype(F32)).astype(o_ref.dtype)


def _diff_attn(qT, k, vT, bias_tiles, lam_params, subln_g, sg, lam_init, t, tk):
    B, _, _, L = qT.shape
    hw = 2 * HEAD_DIM
    nt = L // t
    g_tab = jnp.broadcast_to(subln_g[:, None], (hw, t))
    kspec = lambda c: pl.BlockSpec((1, 1, L, HEAD_DIM), lambda b, h, i: (b, 2 * h + c, 0, 0))
    y_spec = pl.BlockSpec((1, t, hw), lambda b, h, i: (b, i, h))
    y_shape = jax.ShapeDtypeStruct((B, L, DIFF_HEADS * hw), BF16)
    jump_spec = pl.BlockSpec((1, 1, 2 * t), lambda b, h, i: ((b * DIFF_HEADS + h) * nt + i, 0, 0))
    jump_shape = jax.ShapeDtypeStruct((B * DIFF_HEADS * nt, 1, 2 * t), F32)

    def run(lagged):
        out = pl.pallas_call(
            functools.partial(_diff_kernel, lagged, t, tk, lam_init),
            grid=(B, DIFF_HEADS, nt),
            in_specs=[pl.BlockSpec((1, 1, hw, t), lambda b, h, i: (b, h, 0, i)),
                      kspec(0), kspec(1),
                      pl.BlockSpec((1, 1, hw + ONES_ROWS, L), lambda b, h, i: (b, h, 0, 0)),
                      pl.BlockSpec((1, 5, t, t), lambda b, h, i: (h, 0, 0, 0)),
                      pl.BlockSpec((4, HEAD_DIM), lambda b, h, i: (0, 0)),
                      pl.BlockSpec((hw, t), lambda b, h, i: (0, 0)),
                      y_spec],
            out_specs=(y_spec, jump_spec) if lagged else y_spec,
            out_shape=(y_shape, jump_shape) if lagged else y_shape,
            scratch_shapes=_flash_scratch(lagged, hw, tk, 2 * t),
            compiler_params=_cparams(("parallel", "parallel", "parallel")),
            name="diff_attn" if lagged else "diff_attn_exact",
        )(qT, k, k, vT, bias_tiles, lam_params, g_tab, sg)
        return out if lagged else (out,)

    return _with_exact_fallback(run)[0]


def _merge_kernel(final, x_ref, ng_ref, yh_ref, yg_ref, yd_ref, wm_ref, bm_ref, wh_ref, wg_ref, wd_ref,
                  wo_ref, fg_ref, o_ref):
    x = x_ref[...]
    ms = jnp.mean(x * x, axis=-1, keepdims=True)
    h = (x * lax.rsqrt(ms + EPS) * ng_ref[...]).astype(BF16)
    merged = None
    for b, (y_ref, w_ref) in enumerate(((yh_ref, wh_ref), (yg_ref, wg_ref), (yd_ref, wd_ref))):
        z = _dotf(h, wm_ref[:, b * D_MODEL:(b + 1) * D_MODEL]) + bm_ref[:, b * D_MODEL:(b + 1) * D_MODEL]
        gate = 1.0 / (1.0 + jnp.exp(-z))
        term = gate * _dotf(y_ref[...], w_ref[...])
        merged = term if merged is None else merged + term
    y = x + _dotf(merged.astype(BF16), wo_ref[...])
    if final:
        ms = jnp.mean(y * y, axis=-1, keepdims=True)
        y = y * lax.rsqrt(ms + EPS) * fg_ref[...]
    o_ref[...] = y


def _merge_out(x2, norm_g, y_hy, y_gq, y_df, w_merge, b_merge, w_hy, w_gq, w_df, w_out, final_g, final, tm):
    T = x2.shape[0]
    full = lambda a: pl.BlockSpec(a.shape, lambda i: (0,) * a.ndim)
    tok = lambda w: pl.BlockSpec((tm, w), lambda i: (i, 0))
    ng = norm_g.reshape(1, D_MODEL)
    bm = b_merge.reshape(1, -1)
    fg = final_g.reshape(1, D_MODEL)
    return pl.pallas_call(
        functools.partial(_merge_kernel, final),
        grid=(T // tm,),
        in_specs=[tok(D_MODEL), full(ng), tok(HY_WIDTH), tok(GQA_HEADS * HEAD_DIM), tok(DIFF_HEADS * 2 * HEAD_DIM),
                  full(w_merge), full(bm), full(w_hy), full(w_gq), full(w_df), full(w_out), full(fg)],
        out_specs=tok(D_MODEL),
        out_shape=jax.ShapeDtypeStruct((T, D_MODEL), F32),
        compiler_params=_cparams(("parallel",)),
        name="merge_out",
    )(x2, ng, y_hy, y_gq, y_df, w_merge, bm, w_hy, w_gq, w_df, w_out, fg)


def _rope_tables_t(L):
    rows = L // GRID_W
    row = jnp.broadcast_to(jnp.arange(rows, dtype=F32)[:, None], (rows, GRID_W)).reshape(L)
    col = jnp.broadcast_to(jnp.arange(GRID_W, dtype=F32)[None, :], (rows, GRID_W)).reshape(L)
    n_freq = HEAD_DIM // 4
    inv_freq = ROPE_THETA ** (-jnp.arange(n_freq, dtype=F32) / n_freq)
    ang = jnp.concatenate([row[:, None] * inv_freq, col[:, None] * inv_freq], axis=-1)
    return jnp.cos(ang).T, jnp.sin(ang).T


def _hyena_positions(L):
    t01 = jnp.linspace(0.0, 1.0, L, dtype=F32)[:, None]
    bands = (FILT_EMB - 1) // 2
    w = 2.0 * math.pi * jnp.arange(L, dtype=F32)[:, None] / L
    f = jnp.linspace(1e-4, bands - 1, bands, dtype=F32)[None, :]
    z = jnp.concatenate([t01, jnp.cos(f * w), -jnp.sin(f * w)], axis=-1)
    z = jnp.pad(z, ((0, 0), (0, FILT_EMB_PAD - FILT_EMB)))
    max_decay = math.log(HY_TARGET) / HY_FAST_DECAY
    min_decay = math.log(HY_TARGET) / HY_SLOW_DECAY
    deltas = jnp.linspace(min_decay, max_decay, HY_WIDTH, dtype=F32)
    window = jnp.exp(-t01 * jnp.abs(deltas)[None, :]) + HY_SHIFT
    return z, window


def _tile(L, want):
    t = min(L, want)
    assert L % t == 0
    return t


def _trunk(x, p, bias_tiles, attn_t):
    B, L, _ = x.shape
    depth = p['w_in'].shape[0]
    cos_t, sin_t = _rope_tables_t(L)
    z_pad, window = _hyena_positions(L)
    plan = _FftPlan(L)
    tm = _tile(L, 256)
    x2 = x.reshape(B * L, D_MODEL)
    for l in range(depth):
        (hy_u, hy_sg, gq_qT, gq_k, gq_vT, gq_sg, df_qT, df_k, df_vT, df_sg) = _in_proj(
            x2.reshape(B, L, D_MODEL), p['norm_g'][l], p['w_in_bf'][l], p['q_norm_g'][l], p['k_norm_g'][l],
            cos_t, sin_t, tm)

        hv, x0g = _hy_pre(hy_u, p['hy_conv_w'][l], p['hy_conv_b'][l], hy_sg, _tile(L, 512))
        w1_pad = jnp.pad(p['hy_f_w1'][l], ((0, FILT_EMB_PAD - FILT_EMB), (0, 0)))
        kf, kb = _hy_filter(z_pad, window, w1_pad, p['hy_f_b1'][l], p['hy_f_w2'][l], p['hy_f_b2'][l],
                            p['hy_f_wout'][l], p['hy_f_freq'][l], _tile(L, 512))
        spec = _hy_spectrum(plan, kf, kb)
        y_hy = _hy_conv(plan, hv, spec, x0g, p['hy_bias'][l])

        y_gq = _gqa_attn(gq_qT, gq_k, gq_vT, gq_sg, _tile(L, 128), _tile(L // 2, 512))

        lam_init = 0.8 - 0.6 * math.exp(-0.3 * l)
        lam_params = jnp.stack([p['lam_q1'][l], p['lam_k1'][l], p['lam_q2'][l], p['lam_k2'][l]])
        y_df = _diff_attn(df_qT, df_k, df_vT, bias_tiles, lam_params, p['diff_subln_g'][l], df_sg, lam_init,
                          attn_t, max(attn_t, _tile(L // 2, 512)))

        x2 = _merge_out(x2, p['norm_g'][l], y_hy.reshape(B * L, -1), y_gq.reshape(B * L, -1),
                        y_df.reshape(B * L, -1), p['w_merge_bf'][l], p['b_merge'][l], p['w_branch_hy_bf'][l],
                        p['w_branch_gqa_bf'][l], p['w_branch_diff_bf'][l], p['w_out_bf'][l], p['final_g'],
                        l == depth - 1, _tile(B * L, 512))
    return x2.reshape(B, L, D_MODEL)


def kernel(x_prompt, x_sample, rel_bias, norm_g, w_in, hy_conv_w, hy_conv_b, hy_f_w1, hy_f_b1, hy_f_w2, hy_f_b2, hy_f_wout, hy_f_freq, hy_bias, q_norm_g, k_norm_g, lam_q1, lam_k1, lam_q2, lam_k2, diff_subln_g, w_branch_hy, w_branch_gqa, w_branch_diff, w_merge, b_merge, w_out, final_g):
    p = dict(norm_g=norm_g, hy_conv_w=hy_conv_w, hy_conv_b=hy_conv_b, hy_f_w1=hy_f_w1,
             hy_f_b1=hy_f_b1, hy_f_w2=hy_f_w2, hy_f_b2=hy_f_b2, hy_f_wout=hy_f_wout, hy_f_freq=hy_f_freq,
             hy_bias=hy_bias, q_norm_g=q_norm_g, k_norm_g=k_norm_g, lam_q1=lam_q1, lam_k1=lam_k1,
             lam_q2=lam_q2, lam_k2=lam_k2, diff_subln_g=diff_subln_g, b_merge=b_merge, final_g=final_g,
             w_in=w_in,
             w_in_bf=w_in.astype(BF16), w_merge_bf=w_merge.astype(BF16), w_out_bf=w_out.astype(BF16),
             w_branch_hy_bf=w_branch_hy.astype(BF16), w_branch_gqa_bf=w_branch_gqa.astype(BF16),
             w_branch_diff_bf=w_branch_diff.astype(BF16))
    outs = []
    for x in (x_prompt, x_sample):
        attn_t = _tile(x.shape[1], 256)
        bias_tiles = _t5_bias_tiles(rel_bias, attn_t)
        outs.append(_trunk(x, p, bias_tiles, attn_t))
    return tuple(outs)
```

```python
import functools
import math

import numpy as np
import jax
import jax.numpy as jnp
from jax import lax
from jax.experimental import pallas as pl
from jax.experimental.pallas import tpu as pltpu

D_MODEL = 1024
HEAD_DIM = 64
EPS = 1e-6
GRID_W = 64
ROPE_THETA = 10000.0

HY_WIDTH = 512
FILT_EMB = 33
FILT_EMB_PAD = 64
FILT_ORDER = 64
FILT_INNER = 2
HY_FAST_DECAY = 0.3
HY_SLOW_DECAY = 1.5
HY_TARGET = 1e-2
HY_SHIFT = 0.0

GQA_HEADS = 8
GQA_KV_HEADS = 2
GQA_GROUP = GQA_HEADS // GQA_KV_HEADS
DIFF_HEADS = 4
REL_BUCKETS = 32
REL_MAX_DIST = 128

C_HY_U = 0
C_HY_G = 1536
C_GQ_Q = 2048
C_GQ_K = 2560
C_GQ_V = 2688
C_GQ_G = 2816
C_DF_Q = 3328
C_DF_K = 3840
C_DF_V = 4352
C_DF_G = 4864
IN_COLS = 5376

LANES = 128
ONES_ROWS = 16
VMEM_LIMIT = 56 * 1024 * 1024
LOG2E = math.log2(math.e)
HI = lax.Precision.HIGHEST
F32 = jnp.float32
BF16 = jnp.bfloat16


def _cparams(sem):
    return pltpu.CompilerParams(dimension_semantics=sem, vmem_limit_bytes=VMEM_LIMIT)


def _silu(x):
    return x * (1.0 / (1.0 + jnp.exp(-x)))


def _dotf(a, b):
    return jnp.dot(a, b, preferred_element_type=F32)


def _norm_rope_t(xt, g_tab, cos, sin):
    ms = jnp.mean(xt * xt, axis=1, keepdims=True)
    xn = xt * lax.rsqrt(ms + EPS) * g_tab[None]
    half = HEAD_DIM // 2
    x1, x2 = xn[:, :half, :], xn[:, half:, :]
    c, s = cos[None], sin[None]
    return jnp.concatenate([x1 * c - x2 * s, x2 * c + x1 * s], axis=1)


def _in_proj_kernel(tq, x_ref, ng_ref, w_ref, qg_ref, kg_ref, cos_ref, sin_ref,
                    hyu_ref, hysg_ref, gqq_ref, gqk_ref, gqv_ref, gqsg_ref,
                    dfq_ref, dfk_ref, dfv_ref, dfsg_ref):
    x = x_ref[0]
    tm = x.shape[0]
    ms = jnp.mean(x * x, axis=-1, keepdims=True)
    h = (x * lax.rsqrt(ms + EPS) * ng_ref[...]).astype(BF16)

    def proj(lo, hi):
        return _dotf(h, w_ref[:, lo:hi])

    hyu_ref[0] = proj(C_HY_U, C_HY_G)
    hysg_ref[0] = _silu(proj(C_HY_G, C_GQ_Q)).astype(BF16)

    cos, sin = cos_ref[...], sin_ref[...]
    scale = HEAD_DIM ** -0.5
    qt = proj(C_GQ_Q, C_GQ_K).T.reshape(GQA_HEADS, HEAD_DIM, tm)
    qn = (_norm_rope_t(qt, qg_ref[...], cos, sin) * (scale * LOG2E)).astype(BF16)
    for g in range(GQA_KV_HEADS):
        for a in range(tm // tq):
            for hh in range(GQA_GROUP):
                c0 = (a * GQA_GROUP + hh) * tq
                gqq_ref[0, g, :, c0:c0 + tq] = qn[g * GQA_GROUP + hh][:, a * tq:(a + 1) * tq]
    kt = proj(C_GQ_K, C_GQ_V).T.reshape(GQA_KV_HEADS, HEAD_DIM, tm)
    kt = _norm_rope_t(kt, kg_ref[...], cos, sin)
    for g in range(GQA_KV_HEADS):
        gqk_ref[0, g] = kt[g].T.astype(BF16)
    ones_rows = (lax.broadcasted_iota(jnp.int32, (ONES_ROWS, tm), 0) == 0).astype(BF16)
    vt = proj(C_GQ_V, C_GQ_G).T.astype(BF16)
    for g in range(GQA_KV_HEADS):
        gqv_ref[0, g, :HEAD_DIM, :] = vt[g * HEAD_DIM:(g + 1) * HEAD_DIM]
        gqv_ref[0, g, HEAD_DIM:, :] = ones_rows
    gqsg_ref[0] = _silu(proj(C_GQ_G, C_DF_Q)).astype(BF16)

    dfq_ref[0] = (proj(C_DF_Q, C_DF_K) * (scale * LOG2E)).T.reshape(DIFF_HEADS, 2 * HEAD_DIM, tm).astype(BF16)
    dk = proj(C_DF_K, C_DF_V).astype(BF16)
    for j in range(2 * DIFF_HEADS):
        dfk_ref[0, j] = dk[:, j * HEAD_DIM:(j + 1) * HEAD_DIM]
    dvt = proj(C_DF_V, C_DF_G).T.astype(BF16)
    for hh in range(DIFF_HEADS):
        dfv_ref[0, hh, :2 * HEAD_DIM, :] = dvt[hh * 2 * HEAD_DIM:(hh + 1) * 2 * HEAD_DIM]
        dfv_ref[0, hh, 2 * HEAD_DIM:, :] = ones_rows
    dfsg_ref[0] = _silu(proj(C_DF_G, IN_COLS)).astype(BF16)


def _in_proj(x, norm_g, w_in_bf, q_g, k_g, cos_t, sin_t, tm, tq):
    B, L, _ = x.shape
    assert tm % tq == 0
    nt = L // tm
    qg_tab = jnp.broadcast_to(q_g[:, None], (HEAD_DIM, tm))
    kg_tab = jnp.broadcast_to(k_g[:, None], (HEAD_DIM, tm))
    full = lambda shape: pl.BlockSpec(shape, lambda b, i: (0,) * len(shape))
    out_shapes = (
        jax.ShapeDtypeStruct((B, L, 3 * HY_WIDTH), F32),
        jax.ShapeDtypeStruct((B, L, HY_WIDTH), BF16),
        jax.ShapeDtypeStruct((B, GQA_KV_HEADS, HEAD_DIM, GQA_GROUP * L), BF16),
        jax.ShapeDtypeStruct((B, GQA_KV_HEADS, L, HEAD_DIM), BF16),
        jax.ShapeDtypeStruct((B, GQA_KV_HEADS, HEAD_DIM + ONES_ROWS, L), BF16),
        jax.ShapeDtypeStruct((B, L, GQA_HEADS * HEAD_DIM), BF16),
        jax.ShapeDtypeStruct((B, DIFF_HEADS, 2 * HEAD_DIM, L), BF16),
        jax.ShapeDtypeStruct((B, 2 * DIFF_HEADS, L, HEAD_DIM), BF16),
        jax.ShapeDtypeStruct((B, DIFF_HEADS, 2 * HEAD_DIM + ONES_ROWS, L), BF16),
        jax.ShapeDtypeStruct((B, L, DIFF_HEADS * 2 * HEAD_DIM), BF16),
    )
    tok = lambda w: pl.BlockSpec((1, tm, w), lambda b, i: (b, i, 0))
    tr = lambda h, d: pl.BlockSpec((1, h, d, tm), lambda b, i: (b, 0, 0, i))
    rows = lambda h: pl.BlockSpec((1, h, tm, HEAD_DIM), lambda b, i: (b, 0, i, 0))
    return pl.pallas_call(
        functools.partial(_in_proj_kernel, tq),
        grid=(B, nt),
        in_specs=[
            tok(D_MODEL),
            full((1, D_MODEL)),
            full((D_MODEL, IN_COLS)),
            full((HEAD_DIM, tm)),
            full((HEAD_DIM, tm)),
            pl.BlockSpec((HEAD_DIM // 2, tm), lambda b, i: (0, i)),
            pl.BlockSpec((HEAD_DIM // 2, tm), lambda b, i: (0, i)),
        ],
        out_specs=(
            tok(3 * HY_WIDTH), tok(HY_WIDTH),
            pl.BlockSpec((1, GQA_KV_HEADS, HEAD_DIM, GQA_GROUP * tm), lambda b, i: (b, 0, 0, i)),
            rows(GQA_KV_HEADS), tr(GQA_KV_HEADS, HEAD_DIM + ONES_ROWS),
            tok(GQA_HEADS * HEAD_DIM),
            tr(DIFF_HEADS, 2 * HEAD_DIM), rows(2 * DIFF_HEADS), tr(DIFF_HEADS, 2 * HEAD_DIM + ONES_ROWS),
            tok(DIFF_HEADS * 2 * HEAD_DIM),
        ),
        out_shape=out_shapes,
        compiler_params=_cparams(("parallel", "parallel")),
        name="in_proj",
    )(x, norm_g.reshape(1, D_MODEL), w_in_bf, qg_tab, kg_tab, cos_t, sin_t)


def _hy_pre_kernel(u_ref, prev_ref, next_ref, w_ref, b_ref, sg_ref, hv_ref, x0g_ref):
    i = pl.program_id(1)
    nt = pl.num_programs(1)
    u = u_ref[0]
    tl = u.shape[0]
    prev_row = jnp.where(i > 0, prev_ref[0, 0, 7:8, :], 0.0)
    next_row = jnp.where(i < nt - 1, next_ref[0, 0, 0:1, :], 0.0)
    row = lax.broadcasted_iota(jnp.int32, u.shape, 0)
    up = jnp.where(row == 0, prev_row, pltpu.roll(u, 1, 0))
    dn = jnp.where(row == tl - 1, next_row, pltpu.roll(u, tl - 1, 0))
    w = w_ref[...]
    hy = up * w[0:1] + u * w[1:2] + dn * w[2:3] + b_ref[...]
    x0 = hy[:, :HY_WIDTH]
    x1 = hy[:, HY_WIDTH:2 * HY_WIDTH]
    hv = hy[:, 2 * HY_WIDTH:]
    hv_ref[0] = hv * x1
    x0g_ref[0] = (x0 * sg_ref[0].astype(F32)).astype(x0g_ref.dtype)


def _hy_pre(hy_u, conv_w, conv_b, hy_sg, tl):
    B, L, W3 = hy_u.shape
    nt = L // tl
    g8 = tl // 8
    u4 = hy_u.reshape(B, L // 8, 8, W3)
    return pl.pallas_call(
        _hy_pre_kernel,
        grid=(B, nt),
        in_specs=[
            pl.BlockSpec((1, tl, W3), lambda b, i: (b, i, 0)),
            pl.BlockSpec((1, 1, 8, W3), lambda b, i: (b, jnp.maximum(i * g8 - 1, 0), 0, 0)),
            pl.BlockSpec((1, 1, 8, W3), lambda b, i: (b, jnp.minimum((i + 1) * g8, L // 8 - 1), 0, 0)),
            pl.BlockSpec((3, W3), lambda b, i: (0, 0)),
            pl.BlockSpec((1, W3), lambda b, i: (0, 0)),
            pl.BlockSpec((1, tl, HY_WIDTH), lambda b, i: (b, i, 0)),
        ],
        out_specs=(
            pl.BlockSpec((1, tl, HY_WIDTH), lambda b, i: (b, i, 0)),
            pl.BlockSpec((1, tl, HY_WIDTH), lambda b, i: (b, i, 0)),
        ),
        out_shape=(
            jax.ShapeDtypeStruct((B, L, HY_WIDTH), F32),
            jax.ShapeDtypeStruct((B, L, HY_WIDTH), BF16),
        ),
        compiler_params=_cparams(("parallel", "parallel")),
        name="hy_pre",
    )(hy_u, u4, u4, conv_w, conv_b.reshape(1, W3), hy_sg)


def _hy_filter_kernel(z_ref, win_ref, w1_ref, b1_ref, w2_ref, b2_ref, wout_ref, freq_ref, kf_ref, kb_ref):
    i = pl.program_id(0)
    freq = freq_ref[...]
    a = jnp.sin(freq * (jnp.dot(z_ref[...], w1_ref[...], precision=HI, preferred_element_type=F32) + b1_ref[...]))
    for j in range(FILT_INNER):
        a = jnp.sin(freq * (jnp.dot(a, w2_ref[j], precision=HI, preferred_element_type=F32) + b2_ref[j:j + 1, :]))
    hf = jnp.dot(a, wout_ref[...], precision=HI, preferred_element_type=F32)
    win = win_ref[...]
    kf_ref[...] = hf[:, :HY_WIDTH] * win
    row = lax.broadcasted_iota(jnp.int32, win.shape, 0)
    kb_ref[...] = jnp.where((row == 0) & (i == 0), 0.0, hf[:, HY_WIDTH:] * win)


def _hy_filter(z_pad, window, w1_pad, b1, w2, b2, wout, freq, tl):
    L = z_pad.shape[0]
    full = lambda shape: pl.BlockSpec(shape, lambda i: (0,) * len(shape))
    return pl.pallas_call(
        _hy_filter_kernel,
        grid=(L // tl,),
        in_specs=[
            pl.BlockSpec((tl, FILT_EMB_PAD), lambda i: (i, 0)),
            pl.BlockSpec((tl, HY_WIDTH), lambda i: (i, 0)),
            full((FILT_EMB_PAD, FILT_ORDER)),
            full((1, FILT_ORDER)),
            full((FILT_INNER, FILT_ORDER, FILT_ORDER)),
            full((FILT_INNER, FILT_ORDER)),
            full((FILT_ORDER, 2 * HY_WIDTH)),
            full((1, FILT_ORDER)),
        ],
        out_specs=(
            pl.BlockSpec((tl, HY_WIDTH), lambda i: (i, 0)),
            pl.BlockSpec((tl, HY_WIDTH), lambda i: (i, 0)),
        ),
        out_shape=(
            jax.ShapeDtypeStruct((L, HY_WIDTH), F32),
            jax.ShapeDtypeStruct((L, HY_WIDTH), F32),
        ),
        compiler_params=_cparams(("parallel",)),
        name="hy_filter",
    )(z_pad, window, w1_pad, b1.reshape(1, -1), w2, b2, wout, freq.reshape(1, -1))


class _FftPlan:
    def __init__(self, L):
        n = 2 * L
        e = int(round(math.log2(n)))
        assert 2 ** e == n and e >= 8
        self.L, self.n = L, n
        self.n1 = 2 ** (e // 2)
        self.n2 = n // self.n1
        self.h1 = self.n1 // 2
        self.kh = self.h1 + 8
        self.kv = self.h1 + 1
        n1, n2, h1, kh = self.n1, self.n2, self.h1, self.kh
        k1 = np.arange(kh)[:, None]
        a = 2 * np.pi * k1 * np.arange(h1)[None, :] / n1
        self.fa = np.concatenate([np.cos(a), -np.sin(a)], axis=0).astype(np.float32)
        a = 2 * np.pi * np.arange(n2)[:, None] * np.arange(n2)[None, :] / n2
        self.f2r, self.f2i = np.cos(a).astype(np.float32), (-np.sin(a)).astype(np.float32)
        a = 2 * np.pi * k1 * np.arange(n2)[None, :] / n
        self.twr, self.twi = np.cos(a).astype(np.float32), (-np.sin(a)).astype(np.float32)
        wgt = np.where(np.arange(kh) <= h1, 2.0, 0.0)
        wgt[0] = 1.0
        wgt[h1] = 1.0
        a = 2 * np.pi * np.arange(h1)[:, None] * np.arange(kh)[None, :] / n1
        self.e1r = (np.cos(a) * wgt[None, :] / n).astype(np.float32)
        self.e1i = (np.sin(a) * wgt[None, :] / n).astype(np.float32)
        a = 2 * np.pi * np.arange(n2)[:, None] * np.arange(kh)[None, :] / n
        self.e2r, self.e2i = np.cos(a).astype(np.float32), np.sin(a).astype(np.float32)

    def fwd_tables(self):
        return [jnp.asarray(t) for t in (self.fa, self.f2r, self.f2i, self.twr, self.twi)]

    def inv_tables(self):
        return [jnp.asarray(t) for t in (self.e1r, self.e1i, self.e2r, self.e2i)]


def _dot_exact(a, b):
    return jnp.dot(a, b, precision=HI, preferred_element_type=F32)


def _dot_fast(a, b):
    return _dotf(a.astype(BF16), b.astype(BF16))


def _dft_cols(plan, x_ref, fa_ref, pr_ref, pi_ref, dot):
    fa = fa_ref[...]

    def body(n2, _):
        x = x_ref[pl.ds(n2, plan.h1, stride=plan.n2), :]
        y = dot(fa, x)
        pr_ref[pl.ds(n2, plan.kh, stride=plan.n2), :] = y[:plan.kh]
        pi_ref[pl.ds(n2, plan.kh, stride=plan.n2), :] = y[plan.kh:]
        return 0

    lax.fori_loop(0, plan.n2, body, 0, unroll=4)


def _twiddled_row_dft(plan, k1, f2r, f2i, twr_ref, twi_ref, ar, ai, dot):
    twr = twr_ref[pl.ds(k1, 1), :]
    twi = twi_ref[pl.ds(k1, 1), :]
    mr = f2r * twr - f2i * twi
    mi = f2r * twi + f2i * twr
    m = jnp.concatenate([jnp.concatenate([mr, -mi], axis=1), jnp.concatenate([mi, mr], axis=1)], axis=0)
    y = dot(m, jnp.concatenate([ar, ai], axis=0))
    return y[:plan.n2], y[plan.n2:]


def _hy_spectrum_kernel(plan, kf_ref, kb_ref, fa_ref, f2r_ref, f2i_ref, twr_ref, twi_ref, c_ref, pr_ref, pi_ref):
    f2r, f2i = f2r_ref[...], f2i_ref[...]
    n2 = plan.n2
    for which, src in enumerate((kf_ref, kb_ref)):
        _dft_cols(plan, src, fa_ref, pr_ref, pi_ref, _dot_exact)

        def body(k1, _):
            r0 = pl.multiple_of(k1 * n2, n2)
            xr, xi = _twiddled_row_dft(plan, k1, f2r, f2i, twr_ref, twi_ref,
                                       pr_ref[pl.ds(r0, n2), :], pi_ref[pl.ds(r0, n2), :], _dot_exact)
            if which == 0:
                c_ref[0, pl.ds(r0, n2), :] = xr
                c_ref[1, pl.ds(r0, n2), :] = xi
            else:
                c_ref[0, pl.ds(r0, n2), :] = c_ref[0, pl.ds(r0, n2), :] + xr
                c_ref[1, pl.ds(r0, n2), :] = c_ref[1, pl.ds(r0, n2), :] - xi
            return 0

        lax.fori_loop(0, plan.kh, body, 0, unroll=2)


def _hy_spectrum(plan, kf, kb):
    rows = plan.kh * plan.n2
    nc = HY_WIDTH // LANES
    tabs = plan.fwd_tables()
    full = lambda a: pl.BlockSpec(a.shape, lambda c: (0,) * a.ndim)
    return pl.pallas_call(
        functools.partial(_hy_spectrum_kernel, plan),
        grid=(nc,),
        in_specs=[pl.BlockSpec((plan.L, LANES), lambda c: (0, c)),
                  pl.BlockSpec((plan.L, LANES), lambda c: (0, c))] + [full(t) for t in tabs],
        out_specs=pl.BlockSpec((2, rows, LANES), lambda c: (0, 0, c)),
        out_shape=jax.ShapeDtypeStruct((2, rows, HY_WIDTH), F32),
        scratch_shapes=[pltpu.VMEM((rows, LANES), F32), pltpu.VMEM((rows, LANES), F32)],
        compiler_params=_cparams(("parallel",)),
        name="hy_spectrum",
    )(kf, kb, *tabs)


def _hy_conv_kernel(plan, hv_ref, c_ref, x0g_ref, bias_ref, fa_ref, f2r_ref, f2i_ref, twr_ref, twi_ref,
                    e1r_ref, e1i_ref, e2r_ref, e2i_ref, o_ref, pr_ref, pi_ref, y_ref):
    n2, kh, h1 = plan.n2, plan.kh, plan.h1
    x_ref = hv_ref.at[0]
    _dft_cols(plan, x_ref, fa_ref, pr_ref, pi_ref, _dot_fast)

    f2r, f2i = f2r_ref[...], f2i_ref[...]
    inv = jnp.concatenate([jnp.concatenate([f2r, f2i], axis=1), jnp.concatenate([-f2i, f2r], axis=1)],
                          axis=0).astype(BF16)

    def rows_body(k1, _):
        r0 = pl.multiple_of(k1 * n2, n2)
        xr, xi = _twiddled_row_dft(plan, k1, f2r, f2i, twr_ref, twi_ref,
                                   pr_ref[pl.ds(r0, n2), :], pi_ref[pl.ds(r0, n2), :], _dot_fast)
        cr, ci = c_ref[0, pl.ds(r0, n2), :], c_ref[1, pl.ds(r0, n2), :]
        z = jnp.concatenate([xr * cr - xi * ci, xr * ci + xi * cr], axis=0)
        b = _dot_fast(inv, z)
        pr_ref[pl.ds(r0, n2), :] = b[:n2]
        pi_ref[pl.ds(r0, n2), :] = b[n2:]
        return 0

    lax.fori_loop(0, plan.kv, rows_body, 0, unroll=2)

    e1r, e1i = e1r_ref[...], e1i_ref[...]

    def cols_body(j, _):
        e2r = e2r_ref[pl.ds(j, 1), :]
        e2i = e2i_ref[pl.ds(j, 1), :]
        gr = e1r * e2r - e1i * e2i
        gi = e1r * e2i + e1i * e2r
        g = jnp.concatenate([gr, -gi], axis=1)
        b = jnp.concatenate([pr_ref[pl.ds(j, kh, stride=n2), :], pi_ref[pl.ds(j, kh, stride=n2), :]], axis=0)
        y_ref[pl.ds(j, h1, stride=n2), :] = _dot_fast(g, b)
        return 0

    lax.fori_loop(0, n2, cols_body, 0, unroll=4)

    bias = bias_ref[...]
    chunk = min(plan.L, 512)

    def out_body(t, _):
        r0 = pl.multiple_of(t * chunk, chunk)
        hv = hv_ref[0, pl.ds(r0, chunk), :]
        y = y_ref[pl.ds(r0, chunk), :] + hv * bias
        o_ref[0, pl.ds(r0, chunk), :] = (y * x0g_ref[0, pl.ds(r0, chunk), :].astype(F32)).astype(o_ref.dtype)
        return 0

    lax.fori_loop(0, plan.L // chunk, out_body, 0)


def _hy_conv(plan, hv, spec, x0g, hy_bias):
    B, L, _ = hv.shape
    rows = plan.kh * plan.n2
    cb = LANES
    nc = HY_WIDTH // cb
    tabs = plan.fwd_tables() + plan.inv_tables()
    full = lambda a: pl.BlockSpec(a.shape, lambda c, b: (0,) * a.ndim)
    seq = pl.BlockSpec((1, L, cb), lambda c, b: (b, 0, c))
    return pl.pallas_call(
        functools.partial(_hy_conv_kernel, plan),
        grid=(nc, B),
        in_specs=[seq,
                  pl.BlockSpec((2, rows, cb), lambda c, b: (0, 0, c), pipeline_mode=pl.Buffered(1)),
                  seq,
                  pl.BlockSpec((1, cb), lambda c, b: (0, c))] + [full(t) for t in tabs],
        out_specs=seq,
        out_shape=jax.ShapeDtypeStruct((B, L, HY_WIDTH), BF16),
        scratch_shapes=[pltpu.VMEM((rows, cb), F32), pltpu.VMEM((rows, cb), F32),
                        pltpu.VMEM((L, cb), F32)],
        compiler_params=_cparams(("parallel", "parallel")),
        name="hy_conv",
    )(hv, spec, x0g, hy_bias.reshape(1, HY_WIDTH), *tabs)


def _blk(index, size):
    if isinstance(index, int):
        return pl.ds(index * size, size)
    return pl.ds(pl.multiple_of(index * size, size), size)


def _chunks_per_trip(n):
    for u in (8, 4, 2):
        if n % u == 0:
            return u
    raise ValueError(f"need an even number of key chunks, got {n}")


LAG_JUMP_LIMIT = 32.0


def _flash_pipeline(n, score_fn, v_chunk_fn, m_ref, acc_ref, s_refs, mp_refs):
    u = _chunks_per_trip(n)
    m_ref[...] = jnp.full(m_ref.shape, -jnp.inf, F32)
    acc_ref[...] = jnp.zeros(acc_ref.shape, F32)

    def scores(j, slot):
        s = score_fn(j)
        s_refs[slot][...] = s
        mp_refs[slot][...] = jnp.max(s, axis=0, keepdims=True)

    def update(j, slot):
        m_old = m_ref[...]
        m_new = jnp.maximum(m_old, mp_refs[slot][...])
        alpha = jnp.exp2(m_old - m_new)
        p = jnp.exp2(s_refs[slot][...] - m_new).astype(BF16)
        acc_ref[...] = alpha * acc_ref[...] + _dotf(v_chunk_fn(j), p)
        m_ref[...] = m_new

    scores(0, 0)

    def body(jj, _):
        j = u * jj
        for d in range(u):
            scores(j + d + 1 if d + 1 < u else jnp.minimum(j + u, n - 1), (d + 1) % 2)
            update(j + d, d % 2)
        return 0

    lax.fori_loop(0, n // u, body, 0)


def _flash_lagged_tiles(nt, n, score_fn, v_chunk_fn, finish_fn, m_ref, acc_ref, done_ref, jump_ref, s_ref):
    s0 = score_fn(0, 0)
    s_ref[...] = s0
    m_ref[...] = jnp.max(s0, axis=0, keepdims=True)
    acc_ref[...] = jnp.zeros(acc_ref.shape, F32)
    done_ref[...] = jnp.ones(done_ref.shape, F32)
    jump_ref[...] = jnp.zeros(jump_ref.shape, F32)

    def step(j, s):
        m_used = m_ref[...]
        mp = jnp.max(s, axis=0, keepdims=True)
        p = jnp.exp2(s - m_used).astype(BF16)
        m_next = jnp.maximum(m_used, mp)
        alpha = jnp.exp2(m_used - m_next)
        acc_ref[...] = (acc_ref[...] + _dotf(v_chunk_fn(j), p)) * alpha
        jump_ref[...] = jnp.maximum(jump_ref[...], mp - m_used)
        m_ref[...] = m_next

    def body(i, _):
        s_cur = s_ref[...]
        for d in range(n):
            s_next = score_fn(i, d + 1) if d + 1 < n else score_fn(jnp.minimum(i + 1, nt - 1), 0)
            step(d, s_cur)
            if d == 0:
                finish_fn(jnp.maximum(i - 1, 0), done_ref[...])
            s_cur = s_next
        s_ref[...] = s_cur
        done_ref[...] = acc_ref[...]
        acc_ref[...] = jnp.zeros(acc_ref.shape, F32)
        m_ref[...] = jnp.max(s_cur, axis=0, keepdims=True)
        return 0

    lax.fori_loop(0, nt, body, 0)
    finish_fn(nt - 1, done_ref[...])


def _lagged_scratch(rows, tk, nq):
    acc = pltpu.VMEM((rows + ONES_ROWS, nq), F32)
    return [pltpu.VMEM((1, nq), F32), acc, acc, pltpu.VMEM((tk, nq), F32)]


def _exact_scratch(rows, tk, nq):
    return [pltpu.VMEM((1, nq), F32), pltpu.VMEM((rows + ONES_ROWS, nq), F32),
            pltpu.VMEM((tk, nq), F32), pltpu.VMEM((tk, nq), F32),
            pltpu.VMEM((1, nq), F32), pltpu.VMEM((1, nq), F32)]


def _with_exact_fallback(run):
    *outs, jump = run(True)
    outs = tuple(outs)
    return lax.cond(jnp.max(jump) > LAG_JUMP_LIMIT, lambda: tuple(run(False)), lambda: outs)


def _gqa_finish(acc, sg, tq):
    o = acc[:HEAD_DIM] * (1.0 / acc[HEAD_DIM:HEAD_DIM + 1])
    ot = jnp.concatenate([o[:, h * tq:(h + 1) * tq].T for h in range(GQA_GROUP)], axis=1)
    return (ot * sg.astype(F32)).astype(BF16)


def _gqa_lagged_kernel(tq, tk, q_ref, k_ref, v_ref, sg_ref, o_ref, jump_ref, m_ref, acc_ref, done_ref, s_ref):
    L = k_ref.shape[2]
    nq = GQA_GROUP * tq

    def score_fn(i, j):
        qt = q_ref[0, 0, :, _blk(i, nq)]
        return _dotf(k_ref[0, 0, _blk(j, tk), :], qt)

    def v_chunk_fn(j):
        return v_ref[0, 0, :, _blk(j, tk)]

    def finish_fn(i, acc):
        rows = _blk(i, tq)
        o_ref[0, rows, :] = _gqa_finish(acc, sg_ref[0, rows, :], tq)

    _flash_lagged_tiles(L // tq, L // tk, score_fn, v_chunk_fn, finish_fn, m_ref, acc_ref, done_ref,
                        jump_ref.at[0], s_ref)


def _gqa_exact_kernel(tk, q_ref, k_ref, v_ref, sg_ref, o_ref, m_ref, acc_ref, s0_ref, s1_ref, mp0_ref, mp1_ref):
    qt = q_ref[0, 0]
    tq = qt.shape[-1] // GQA_GROUP
    L = k_ref.shape[2]

    def score_fn(j):
        return _dotf(k_ref[0, 0, pl.ds(pl.multiple_of(j * tk, tk), tk), :], qt)

    def v_chunk_fn(j):
        return v_ref[0, 0, :, pl.ds(pl.multiple_of(j * tk, tk), tk)]

    _flash_pipeline(L // tk, score_fn, v_chunk_fn, m_ref, acc_ref, (s0_ref, s1_ref), (mp0_ref, mp1_ref))
    o_ref[0] = _gqa_finish(acc_ref[...], sg_ref[0], tq)


def _gqa_attn(qT, k, vT, sg, tq, tk):
    B, _, _, L4 = qT.shape
    L = L4 // GQA_GROUP
    gw = GQA_GROUP * HEAD_DIM
    nq = GQA_GROUP * tq
    nt = L // tq
    y_shape = jax.ShapeDtypeStruct((B, L, GQA_HEADS * HEAD_DIM), BF16)

    def run(lagged):
        if lagged:
            whole = lambda shape: pl.BlockSpec((1, 1) + shape, lambda b, g: (b, g, 0, 0))
            y_spec = pl.BlockSpec((1, L, gw), lambda b, g: (b, 0, g))
            return pl.pallas_call(
                functools.partial(_gqa_lagged_kernel, tq, tk),
                grid=(B, GQA_KV_HEADS),
                in_specs=[whole((HEAD_DIM, L4)), whole((L, HEAD_DIM)), whole((HEAD_DIM + ONES_ROWS, L)), y_spec],
                out_specs=(y_spec, pl.BlockSpec((1, 1, nq), lambda b, g: (b * GQA_KV_HEADS + g, 0, 0))),
                out_shape=(y_shape, jax.ShapeDtypeStruct((B * GQA_KV_HEADS, 1, nq), F32)),
                scratch_shapes=_lagged_scratch(HEAD_DIM, tk, nq),
                compiler_params=_cparams(("parallel", "parallel")),
                name="gqa_attn",
            )(qT, k, vT, sg)
        y_spec = pl.BlockSpec((1, tq, gw), lambda b, g, i: (b, i, g))
        return (pl.pallas_call(
            functools.partial(_gqa_exact_kernel, tk),
            grid=(B, GQA_KV_HEADS, nt),
            in_specs=[pl.BlockSpec((1, 1, HEAD_DIM, nq), lambda b, g, i: (b, g, 0, i)),
                      pl.BlockSpec((1, 1, L, HEAD_DIM), lambda b, g, i: (b, g, 0, 0)),
                      pl.BlockSpec((1, 1, HEAD_DIM + ONES_ROWS, L), lambda b, g, i: (b, g, 0, 0)),
                      y_spec],
            out_specs=y_spec,
            out_shape=y_shape,
            scratch_shapes=_exact_scratch(HEAD_DIM, tk, nq),
            compiler_params=_cparams(("parallel", "parallel", "parallel")),
            name="gqa_attn_exact",
        )(qT, k, vT, sg),)

    return _with_exact_fallback(run)[0]


def _t5_bias_kernel(t, rb_ref, o_ref):
    h = pl.program_id(0)
    kk = lax.broadcasted_iota(jnp.int32, (t, t), 0)
    qq = lax.broadcasted_iota(jnp.int32, (t, t), 1)
    nb = REL_BUCKETS // 2
    max_exact = nb // 2
    thresholds = [int(math.ceil(max_exact * (REL_MAX_DIST / max_exact) ** (j / (nb - max_exact)) - 1e-9))
                  for j in range(1, nb - max_exact)]
    for idx in range(5):
        rel = (idx - 2) * t + kk - qq
        n = jnp.abs(rel)
        large = jnp.full((t, t), max_exact, jnp.int32)
        for th in thresholds:
            large = large + (n >= th).astype(jnp.int32)
        bucket = jnp.where(rel > 0, nb, 0) + jnp.where(n < max_exact, n, large)
        val = jnp.zeros((t, t), F32)
        for b in range(REL_BUCKETS):
            val = jnp.where(bucket == b, rb_ref[h, b], val)
        o_ref[0, idx] = val * LOG2E


def _t5_bias_tiles(rel_bias, t):
    return pl.pallas_call(
        functools.partial(_t5_bias_kernel, t),
        grid=(DIFF_HEADS,),
        in_specs=[pl.BlockSpec(memory_space=pltpu.SMEM)],
        out_specs=pl.BlockSpec((1, 5, t, t), lambda h: (h, 0, 0, 0)),
        out_shape=jax.ShapeDtypeStruct((DIFF_HEADS, 5, t, t), F32),
        compiler_params=_cparams(("parallel",)),
        name="t5_bias_tiles",
    )(rel_bias.T)


def _diff_scores(t, tk, i, j, qt, k0_ref, k1_ref, bias_ref):
    tiles = tk // t
    bias = jnp.concatenate([bias_ref[0, jnp.clip(j * tiles + r - i, -2, 2) + 2] for r in range(tiles)],
                           axis=0)
    return jnp.concatenate(
        [_dotf(k0_ref[0, 0, _blk(j, tk), :], qt[:HEAD_DIM]) + bias,
         _dotf(k1_ref[0, 0, _blk(j, tk), :], qt[HEAD_DIM:]) + bias], axis=1)


def _diff_finish(acc, t, lam_init, lam_ref, g_ref, sg):
    hw = 2 * HEAD_DIM
    lp = lam_ref[...]
    lam = (jnp.exp(jnp.sum(lp[0:1] * lp[1:2], axis=1, keepdims=True))
           - jnp.exp(jnp.sum(lp[2:3] * lp[3:4], axis=1, keepdims=True)) + lam_init)
    o = acc[:hw] * (1.0 / acc[hw:hw + 1])
    o = o[:, :t] - lam * o[:, t:]
    ms = jnp.mean(o * o, axis=0, keepdims=True)
    o = o * lax.rsqrt(ms + EPS) * g_ref[...] * (1.0 - lam_init)
    return (o.T * sg.astype(F32)).astype(BF16)


def _diff_lagged_kernel(t, tk, lam_init, q_ref, k0_ref, k1_ref, v_ref, bias_ref, lam_ref, g_ref, sg_ref,
                        o_ref, jump_ref, m_ref, acc_ref, done_ref, s_ref):
    L = k0_ref.shape[2]

    def score_fn(i, j):
        return _diff_scores(t, tk, i, j, q_ref[0, 0, :, _blk(i, t)], k0_ref, k1_ref, bias_ref)

    def v_chunk_fn(j):
        return v_ref[0, 0, :, _blk(j, tk)]

    def finish_fn(i, acc):
        rows = _blk(i, t)
        o_ref[0, rows, :] = _diff_finish(acc, t, lam_init, lam_ref, g_ref, sg_ref[0, rows, :])

    _flash_lagged_tiles(L // t, L // tk, score_fn, v_chunk_fn, finish_fn, m_ref, acc_ref, done_ref,
                        jump_ref.at[0], s_ref)


def _diff_exact_kernel(t, tk, lam_init, q_ref, k0_ref, k1_ref, v_ref, bias_ref, lam_ref, g_ref, sg_ref,
                       o_ref, m_ref, acc_ref, s0_ref, s1_ref, mp0_ref, mp1_ref):
    i = pl.program_id(2)
    L = k0_ref.shape[2]
    qt = q_ref[0, 0]

    def score_fn(j):
        return _diff_scores(t, tk, i, j, qt, k0_ref, k1_ref, bias_ref)

    def v_chunk_fn(j):
        return v_ref[0, 0, :, _blk(j, tk)]

    _flash_pipeline(L // tk, score_fn, v_chunk_fn, m_ref, acc_ref, (s0_ref, s1_ref), (mp0_ref, mp1_ref))
    o_ref[0] = _diff_finish(acc_ref[...], t, lam_init, lam_ref, g_ref, sg_ref[0])


def _diff_attn(qT, k, vT, bias_tiles, lam_params, subln_g, sg, lam_init, t, tk):
    B, _, _, L = qT.shape
    hw = 2 * HEAD_DIM
    nt = L // t
    g_tab = jnp.broadcast_to(subln_g[:, None], (hw, t))
    y_shape = jax.ShapeDtypeStruct((B, L, DIFF_HEADS * hw), BF16)

    def run(lagged):
        nd = 2 if lagged else 3

        def spec(block, index):
            return pl.BlockSpec(block, (lambda b, h: index(b, h, 0)) if lagged else index)

        tq_blk = L if lagged else t
        y_spec = spec((1, tq_blk, hw), lambda b, h, i: (b, i, h))
        in_specs = [spec((1, 1, hw, tq_blk), lambda b, h, i: (b, h, 0, i)),
                    spec((1, 1, L, HEAD_DIM), lambda b, h, i: (b, 2 * h, 0, 0)),
                    spec((1, 1, L, HEAD_DIM), lambda b, h, i: (b, 2 * h + 1, 0, 0)),
                    spec((1, 1, hw + ONES_ROWS, L), lambda b, h, i: (b, h, 0, 0)),
                    spec((1, 5, t, t), lambda b, h, i: (h, 0, 0, 0)),
                    spec((4, HEAD_DIM), lambda b, h, i: (0, 0)),
                    spec((hw, t), lambda b, h, i: (0, 0)),
                    y_spec]
        args = (qT, k, k, vT, bias_tiles, lam_params, g_tab, sg)
        if lagged:
            return pl.pallas_call(
                functools.partial(_diff_lagged_kernel, t, tk, lam_init),
                grid=(B, DIFF_HEADS),
                in_specs=in_specs,
                out_specs=(y_spec, pl.BlockSpec((1, 1, 2 * t), lambda b, h: (b * DIFF_HEADS + h, 0, 0))),
                out_shape=(y_shape, jax.ShapeDtypeStruct((B * DIFF_HEADS, 1, 2 * t), F32)),
                scratch_shapes=_lagged_scratch(hw, tk, 2 * t),
                compiler_params=_cparams(("parallel",) * nd),
                name="diff_attn",
            )(*args)
        return (pl.pallas_call(
            functools.partial(_diff_exact_kernel, t, tk, lam_init),
            grid=(B, DIFF_HEADS, nt),
            in_specs=in_specs,
            out_specs=y_spec,
            out_shape=y_shape,
            scratch_shapes=_exact_scratch(hw, tk, 2 * t),
            compiler_params=_cparams(("parallel",) * nd),
            name="diff_attn_exact",
        )(*args),)

    return _with_exact_fallback(run)[0]


def _merge_kernel(final, x_ref, ng_ref, yh_ref, yg_ref, yd_ref, wm_ref, bm_ref, wh_ref, wg_ref, wd_ref,
                  wo_ref, fg_ref, o_ref):
    x = x_ref[...]
    ms = jnp.mean(x * x, axis=-1, keepdims=True)
    h = (x * lax.rsqrt(ms + EPS) * ng_ref[...]).astype(BF16)
    merged = None
    for b, (y_ref, w_ref) in enumerate(((yh_ref, wh_ref), (yg_ref, wg_ref), (yd_ref, wd_ref))):
        z = _dotf(h, wm_ref[:, b * D_MODEL:(b + 1) * D_MODEL]) + bm_ref[:, b * D_MODEL:(b + 1) * D_MODEL]
        gate = 1.0 / (1.0 + jnp.exp(-z))
        term = gate * _dotf(y_ref[...], w_ref[...])
        merged = term if merged is None else merged + term
    y = x + _dotf(merged.astype(BF16), wo_ref[...])
    if final:
        ms = jnp.mean(y * y, axis=-1, keepdims=True)
        y = y * lax.rsqrt(ms + EPS) * fg_ref[...]
    o_ref[...] = y


def _merge_out(x2, norm_g, y_hy, y_gq, y_df, w_merge, b_merge, w_hy, w_gq, w_df, w_out, final_g, final, tm):
    T = x2.shape[0]
    full = lambda a: pl.BlockSpec(a.shape, lambda i: (0,) * a.ndim)
    tok = lambda w: pl.BlockSpec((tm, w), lambda i: (i, 0))
    ng = norm_g.reshape(1, D_MODEL)
    bm = b_merge.reshape(1, -1)
    fg = final_g.reshape(1, D_MODEL)
    return pl.pallas_call(
        functools.partial(_merge_kernel, final),
        grid=(T // tm,),
        in_specs=[tok(D_MODEL), full(ng), tok(HY_WIDTH), tok(GQA_HEADS * HEAD_DIM), tok(DIFF_HEADS * 2 * HEAD_DIM),
                  full(w_merge), full(bm), full(w_hy), full(w_gq), full(w_df), full(w_out), full(fg)],
        out_specs=tok(D_MODEL),
        out_shape=jax.ShapeDtypeStruct((T, D_MODEL), F32),
        compiler_params=_cparams(("parallel",)),
        name="merge_out",
    )(x2, ng, y_hy, y_gq, y_df, w_merge, bm, w_hy, w_gq, w_df, w_out, fg)


def _rope_tables_t(L):
    rows = L // GRID_W
    row = jnp.broadcast_to(jnp.arange(rows, dtype=F32)[:, None], (rows, GRID_W)).reshape(L)
    col = jnp.broadcast_to(jnp.arange(GRID_W, dtype=F32)[None, :], (rows, GRID_W)).reshape(L)
    n_freq = HEAD_DIM // 4
    inv_freq = ROPE_THETA ** (-jnp.arange(n_freq, dtype=F32) / n_freq)
    ang = jnp.concatenate([row[:, None] * inv_freq, col[:, None] * inv_freq], axis=-1)
    return jnp.cos(ang).T, jnp.sin(ang).T


def _hyena_positions(L):
    t01 = jnp.linspace(0.0, 1.0, L, dtype=F32)[:, None]
    bands = (FILT_EMB - 1) // 2
    w = 2.0 * math.pi * jnp.arange(L, dtype=F32)[:, None] / L
    f = jnp.linspace(1e-4, bands - 1, bands, dtype=F32)[None, :]
    z = jnp.concatenate([t01, jnp.cos(f * w), -jnp.sin(f * w)], axis=-1)
    z = jnp.pad(z, ((0, 0), (0, FILT_EMB_PAD - FILT_EMB)))
    max_decay = math.log(HY_TARGET) / HY_FAST_DECAY
    min_decay = math.log(HY_TARGET) / HY_SLOW_DECAY
    deltas = jnp.linspace(min_decay, max_decay, HY_WIDTH, dtype=F32)
    window = jnp.exp(-t01 * jnp.abs(deltas)[None, :]) + HY_SHIFT
    return z, window


def _tile(L, want):
    t = min(L, want)
    assert L % t == 0
    return t


def _trunk(x, p, bias_tiles, attn_t):
    B, L, _ = x.shape
    depth = p['w_in'].shape[0]
    cos_t, sin_t = _rope_tables_t(L)
    z_pad, window = _hyena_positions(L)
    plan = _FftPlan(L)
    tm = _tile(L, 256)
    gqa_tq = _tile(L, 128)
    x2 = x.reshape(B * L, D_MODEL)
    for l in range(depth):
        (hy_u, hy_sg, gq_qT, gq_k, gq_vT, gq_sg, df_qT, df_k, df_vT, df_sg) = _in_proj(
            x2.reshape(B, L, D_MODEL), p['norm_g'][l], p['w_in_bf'][l], p['q_norm_g'][l], p['k_norm_g'][l],
            cos_t, sin_t, tm, gqa_tq)

        hv, x0g = _hy_pre(hy_u, p['hy_conv_w'][l], p['hy_conv_b'][l], hy_sg, _tile(L, 512))
        w1_pad = jnp.pad(p['hy_f_w1'][l], ((0, FILT_EMB_PAD - FILT_EMB), (0, 0)))
        kf, kb = _hy_filter(z_pad, window, w1_pad, p['hy_f_b1'][l], p['hy_f_w2'][l], p['hy_f_b2'][l],
                            p['hy_f_wout'][l], p['hy_f_freq'][l], _tile(L, 512))
        spec = _hy_spectrum(plan, kf, kb)
        y_hy = _hy_conv(plan, hv, spec, x0g, p['hy_bias'][l])

        y_gq = _gqa_attn(gq_qT, gq_k, gq_vT, gq_sg, gqa_tq, _tile(L // 2, 512))

        lam_init = 0.8 - 0.6 * math.exp(-0.3 * l)
        lam_params = jnp.stack([p['lam_q1'][l], p['lam_k1'][l], p['lam_q2'][l], p['lam_k2'][l]])
        y_df = _diff_attn(df_qT, df_k, df_vT, bias_tiles, lam_params, p['diff_subln_g'][l], df_sg, lam_init,
                          attn_t, max(attn_t, _tile(L // 2, 512)))

        x2 = _merge_out(x2, p['norm_g'][l], y_hy.reshape(B * L, -1), y_gq.reshape(B * L, -1),
                        y_df.reshape(B * L, -1), p['w_merge_bf'][l], p['b_merge'][l], p['w_branch_hy_bf'][l],
                        p['w_branch_gqa_bf'][l], p['w_branch_diff_bf'][l], p['w_out_bf'][l], p['final_g'],
                        l == depth - 1, _tile(B * L, 512))
    return x2.reshape(B, L, D_MODEL)


def kernel(x_prompt, x_sample, rel_bias, norm_g, w_in, hy_conv_w, hy_conv_b, hy_f_w1, hy_f_b1, hy_f_w2, hy_f_b2, hy_f_wout, hy_f_freq, hy_bias, q_norm_g, k_norm_g, lam_q1, lam_k1, lam_q2, lam_k2, diff_subln_g, w_branch_hy, w_branch_gqa, w_branch_diff, w_merge, b_merge, w_out, final_g):
    p = dict(norm_g=norm_g, hy_conv_w=hy_conv_w, hy_conv_b=hy_conv_b, hy_f_w1=hy_f_w1,
             hy_f_b1=hy_f_b1, hy_f_w2=hy_f_w2, hy_f_b2=hy_f_b2, hy_f_wout=hy_f_wout, hy_f_freq=hy_f_freq,
             hy_bias=hy_bias, q_norm_g=q_norm_g, k_norm_g=k_norm_g, lam_q1=lam_q1, lam_k1=lam_k1,
             lam_q2=lam_q2, lam_k2=lam_k2, diff_subln_g=diff_subln_g, b_merge=b_merge, final_g=final_g,
             w_in=w_in,
             w_in_bf=w_in.astype(BF16), w_merge_bf=w_merge.astype(BF16), w_out_bf=w_out.astype(BF16),
             w_branch_hy_bf=w_branch_hy.astype(BF16), w_branch_gqa_bf=w_branch_gqa.astype(BF16),
             w_branch_diff_bf=w_branch_diff.astype(BF16))
    outs = []
    for x in (x_prompt, x_sample):
        attn_t = _tile(x.shape[1], 256)
        bias_tiles = _t5_bias_tiles(rel_bias, attn_t)
        outs.append(_trunk(x, p, bias_tiles, attn_t))
    return tuple(outs)
```

```python
import functools
import math

import numpy as np
import jax
import jax.numpy as jnp
from jax import lax
from jax.experimental import pallas as pl
from jax.experimental.pallas import tpu as pltpu

D_MODEL = 1024
HEAD_DIM = 64
EPS = 1e-6
GRID_W = 64
ROPE_THETA = 10000.0

HY_WIDTH = 512
FILT_EMB = 33
FILT_EMB_PAD = 64
FILT_ORDER = 64
FILT_INNER = 2
HY_FAST_DECAY = 0.3
HY_SLOW_DECAY = 1.5
HY_TARGET = 1e-2
HY_SHIFT = 0.0

GQA_HEADS = 8
GQA_KV_HEADS = 2
GQA_GROUP = GQA_HEADS // GQA_KV_HEADS
DIFF_HEADS = 4
REL_BUCKETS = 32
REL_MAX_DIST = 128

C_HY_U = 0
C_HY_G = 1536
C_GQ_Q = 2048
C_GQ_K = 2560
C_GQ_V = 2688
C_GQ_G = 2816
C_DF_Q = 3328
C_DF_K = 3840
C_DF_V = 4352
C_DF_G = 4864
IN_COLS = 5376

LANES = 128
ONES_ROWS = 16
VMEM_LIMIT = 56 * 1024 * 1024
LOG2E = math.log2(math.e)
HI = lax.Precision.HIGHEST
F32 = jnp.float32
BF16 = jnp.bfloat16


def _cparams(sem):
    return pltpu.CompilerParams(dimension_semantics=sem, vmem_limit_bytes=VMEM_LIMIT)


def _silu(x):
    return x * (1.0 / (1.0 + jnp.exp(-x)))


def _dotf(a, b):
    return jnp.dot(a, b, preferred_element_type=F32)


def _norm_rope_t(xt, g_tab, cos, sin):
    ms = jnp.mean(xt * xt, axis=1, keepdims=True)
    xn = xt * lax.rsqrt(ms + EPS) * g_tab[None]
    half = HEAD_DIM // 2
    x1, x2 = xn[:, :half, :], xn[:, half:, :]
    c, s = cos[None], sin[None]
    return jnp.concatenate([x1 * c - x2 * s, x2 * c + x1 * s], axis=1)


def _in_proj_kernel(tq, x_ref, ng_ref, w_ref, qg_ref, kg_ref, cos_ref, sin_ref,
                    hyu_ref, hysg_ref, gqq_ref, gqk_ref, gqv_ref, gqsg_ref,
                    dfq_ref, dfk_ref, dfv_ref, dfsg_ref):
    x = x_ref[0]
    tm = x.shape[0]
    ms = jnp.mean(x * x, axis=-1, keepdims=True)
    h = (x * lax.rsqrt(ms + EPS) * ng_ref[...]).astype(BF16)

    def proj(lo, hi):
        return _dotf(h, w_ref[:, lo:hi])

    hyu_ref[0] = proj(C_HY_U, C_HY_G)
    hysg_ref[0] = _silu(proj(C_HY_G, C_GQ_Q)).astype(BF16)

    cos, sin = cos_ref[...], sin_ref[...]
    scale = HEAD_DIM ** -0.5
    qt = proj(C_GQ_Q, C_GQ_K).T.reshape(GQA_HEADS, HEAD_DIM, tm)
    qn = (_norm_rope_t(qt, qg_ref[...], cos, sin) * (scale * LOG2E)).astype(BF16)
    for g in range(GQA_KV_HEADS):
        for a in range(tm // tq):
            for hh in range(GQA_GROUP):
                c0 = (a * GQA_GROUP + hh) * tq
                gqq_ref[0, g, :, c0:c0 + tq] = qn[g * GQA_GROUP + hh][:, a * tq:(a + 1) * tq]
    kt = proj(C_GQ_K, C_GQ_V).T.reshape(GQA_KV_HEADS, HEAD_DIM, tm)
    kt = _norm_rope_t(kt, kg_ref[...], cos, sin)
    for g in range(GQA_KV_HEADS):
        gqk_ref[0, g] = kt[g].T.astype(BF16)
    ones_rows = (lax.broadcasted_iota(jnp.int32, (ONES_ROWS, tm), 0) == 0).astype(BF16)
    vt = proj(C_GQ_V, C_GQ_G).T.astype(BF16)
    for g in range(GQA_KV_HEADS):
        gqv_ref[0, g, :HEAD_DIM, :] = vt[g * HEAD_DIM:(g + 1) * HEAD_DIM]
        gqv_ref[0, g, HEAD_DIM:, :] = ones_rows
    gqsg_ref[0] = _silu(proj(C_GQ_G, C_DF_Q)).astype(BF16)

    dfq_ref[0] = (proj(C_DF_Q, C_DF_K) * (scale * LOG2E)).T.reshape(DIFF_HEADS, 2 * HEAD_DIM, tm).astype(BF16)
    dk = proj(C_DF_K, C_DF_V).astype(BF16)
    for j in range(2 * DIFF_HEADS):
        dfk_ref[0, j] = dk[:, j * HEAD_DIM:(j + 1) * HEAD_DIM]
    dvt = proj(C_DF_V, C_DF_G).T.astype(BF16)
    for hh in range(DIFF_HEADS):
        dfv_ref[0, hh, :2 * HEAD_DIM, :] = dvt[hh * 2 * HEAD_DIM:(hh + 1) * 2 * HEAD_DIM]
        dfv_ref[0, hh, 2 * HEAD_DIM:, :] = ones_rows
    dfsg_ref[0] = _silu(proj(C_DF_G, IN_COLS)).astype(BF16)


def _in_proj(x, norm_g, w_in_bf, q_g, k_g, cos_t, sin_t, tm, tq):
    B, L, _ = x.shape
    assert tm % tq == 0
    nt = L // tm
    qg_tab = jnp.broadcast_to(q_g[:, None], (HEAD_DIM, tm))
    kg_tab = jnp.broadcast_to(k_g[:, None], (HEAD_DIM, tm))
    full = lambda shape: pl.BlockSpec(shape, lambda b, i: (0,) * len(shape))
    out_shapes = (
        jax.ShapeDtypeStruct((B, L, 3 * HY_WIDTH), F32),
        jax.ShapeDtypeStruct((B, L, HY_WIDTH), BF16),
        jax.ShapeDtypeStruct((B, GQA_KV_HEADS, HEAD_DIM, GQA_GROUP * L), BF16),
        jax.ShapeDtypeStruct((B, GQA_KV_HEADS, L, HEAD_DIM), BF16),
        jax.ShapeDtypeStruct((B, GQA_KV_HEADS, HEAD_DIM + ONES_ROWS, L), BF16),
        jax.ShapeDtypeStruct((B, L, GQA_HEADS * HEAD_DIM), BF16),
        jax.ShapeDtypeStruct((B, DIFF_HEADS, 2 * HEAD_DIM, L), BF16),
        jax.ShapeDtypeStruct((B, 2 * DIFF_HEADS, L, HEAD_DIM), BF16),
        jax.ShapeDtypeStruct((B, DIFF_HEADS, 2 * HEAD_DIM + ONES_ROWS, L), BF16),
        jax.ShapeDtypeStruct((B, L, DIFF_HEADS * 2 * HEAD_DIM), BF16),
    )
    tok = lambda w: pl.BlockSpec((1, tm, w), lambda b, i: (b, i, 0))
    tr = lambda h, d: pl.BlockSpec((1, h, d, tm), lambda b, i: (b, 0, 0, i))
    rows = lambda h: pl.BlockSpec((1, h, tm, HEAD_DIM), lambda b, i: (b, 0, i, 0))
    return pl.pallas_call(
        functools.partial(_in_proj_kernel, tq),
        grid=(B, nt),
        in_specs=[
            tok(D_MODEL),
            full((1, D_MODEL)),
            full((D_MODEL, IN_COLS)),
            full((HEAD_DIM, tm)),
            full((HEAD_DIM, tm)),
            pl.BlockSpec((HEAD_DIM // 2, tm), lambda b, i: (0, i)),
            pl.BlockSpec((HEAD_DIM // 2, tm), lambda b, i: (0, i)),
        ],
        out_specs=(
            tok(3 * HY_WIDTH), tok(HY_WIDTH),
            pl.BlockSpec((1, GQA_KV_HEADS, HEAD_DIM, GQA_GROUP * tm), lambda b, i: (b, 0, 0, i)),
            rows(GQA_KV_HEADS), tr(GQA_KV_HEADS, HEAD_DIM + ONES_ROWS),
            tok(GQA_HEADS * HEAD_DIM),
            tr(DIFF_HEADS, 2 * HEAD_DIM), rows(2 * DIFF_HEADS), tr(DIFF_HEADS, 2 * HEAD_DIM + ONES_ROWS),
            tok(DIFF_HEADS * 2 * HEAD_DIM),
        ),
        out_shape=out_shapes,
        compiler_params=_cparams(("parallel", "parallel")),
        name="in_proj",
    )(x, norm_g.reshape(1, D_MODEL), w_in_bf, qg_tab, kg_tab, cos_t, sin_t)


def _hy_pre_kernel(u_ref, prev_ref, next_ref, w_ref, b_ref, sg_ref, hv_ref, x0g_ref):
    i = pl.program_id(1)
    nt = pl.num_programs(1)
    u = u_ref[0]
    tl = u.shape[0]
    prev_row = jnp.where(i > 0, prev_ref[0, 0, 7:8, :], 0.0)
    next_row = jnp.where(i < nt - 1, next_ref[0, 0, 0:1, :], 0.0)
    row = lax.broadcasted_iota(jnp.int32, u.shape, 0)
    up = jnp.where(row == 0, prev_row, pltpu.roll(u, 1, 0))
    dn = jnp.where(row == tl - 1, next_row, pltpu.roll(u, tl - 1, 0))
    w = w_ref[...]
    hy = up * w[0:1] + u * w[1:2] + dn * w[2:3] + b_ref[...]
    x0 = hy[:, :HY_WIDTH]
    x1 = hy[:, HY_WIDTH:2 * HY_WIDTH]
    hv = hy[:, 2 * HY_WIDTH:]
    hv_ref[0] = hv * x1
    x0g_ref[0] = (x0 * sg_ref[0].astype(F32)).astype(x0g_ref.dtype)


def _hy_pre(hy_u, conv_w, conv_b, hy_sg, tl):
    B, L, W3 = hy_u.shape
    nt = L // tl
    g8 = tl // 8
    u4 = hy_u.reshape(B, L // 8, 8, W3)
    return pl.pallas_call(
        _hy_pre_kernel,
        grid=(B, nt),
        in_specs=[
            pl.BlockSpec((1, tl, W3), lambda b, i: (b, i, 0)),
            pl.BlockSpec((1, 1, 8, W3), lambda b, i: (b, jnp.maximum(i * g8 - 1, 0), 0, 0)),
            pl.BlockSpec((1, 1, 8, W3), lambda b, i: (b, jnp.minimum((i + 1) * g8, L // 8 - 1), 0, 0)),
            pl.BlockSpec((3, W3), lambda b, i: (0, 0)),
            pl.BlockSpec((1, W3), lambda b, i: (0, 0)),
            pl.BlockSpec((1, tl, HY_WIDTH), lambda b, i: (b, i, 0)),
        ],
        out_specs=(
            pl.BlockSpec((1, tl, HY_WIDTH), lambda b, i: (b, i, 0)),
            pl.BlockSpec((1, tl, HY_WIDTH), lambda b, i: (b, i, 0)),
        ),
        out_shape=(
            jax.ShapeDtypeStruct((B, L, HY_WIDTH), F32),
            jax.ShapeDtypeStruct((B, L, HY_WIDTH), BF16),
        ),
        compiler_params=_cparams(("parallel", "parallel")),
        name="hy_pre",
    )(hy_u, u4, u4, conv_w, conv_b.reshape(1, W3), hy_sg)


def _hy_filter_kernel(z_ref, win_ref, w1_ref, b1_ref, w2_ref, b2_ref, wout_ref, freq_ref, kf_ref, kb_ref):
    i = pl.program_id(0)
    freq = freq_ref[...]
    a = jnp.sin(freq * (jnp.dot(z_ref[...], w1_ref[...], precision=HI, preferred_element_type=F32) + b1_ref[...]))
    for j in range(FILT_INNER):
        a = jnp.sin(freq * (jnp.dot(a, w2_ref[j], precision=HI, preferred_element_type=F32) + b2_ref[j:j + 1, :]))
    hf = jnp.dot(a, wout_ref[...], precision=HI, preferred_element_type=F32)
    win = win_ref[...]
    kf_ref[...] = hf[:, :HY_WIDTH] * win
    row = lax.broadcasted_iota(jnp.int32, win.shape, 0)
    kb_ref[...] = jnp.where((row == 0) & (i == 0), 0.0, hf[:, HY_WIDTH:] * win)


def _hy_filter(z_pad, window, w1_pad, b1, w2, b2, wout, freq, tl):
    L = z_pad.shape[0]
    full = lambda shape: pl.BlockSpec(shape, lambda i: (0,) * len(shape))
    return pl.pallas_call(
        _hy_filter_kernel,
        grid=(L // tl,),
        in_specs=[
            pl.BlockSpec((tl, FILT_EMB_PAD), lambda i: (i, 0)),
            pl.BlockSpec((tl, HY_WIDTH), lambda i: (i, 0)),
            full((FILT_EMB_PAD, FILT_ORDER)),
            full((1, FILT_ORDER)),
            full((FILT_INNER, FILT_ORDER, FILT_ORDER)),
            full((FILT_INNER, FILT_ORDER)),
            full((FILT_ORDER, 2 * HY_WIDTH)),
            full((1, FILT_ORDER)),
        ],
        out_specs=(
            pl.BlockSpec((tl, HY_WIDTH), lambda i: (i, 0)),
            pl.BlockSpec((tl, HY_WIDTH), lambda i: (i, 0)),
        ),
        out_shape=(
            jax.ShapeDtypeStruct((L, HY_WIDTH), F32),
            jax.ShapeDtypeStruct((L, HY_WIDTH), F32),
        ),
        compiler_params=_cparams(("parallel",)),
        name="hy_filter",
    )(z_pad, window, w1_pad, b1.reshape(1, -1), w2, b2, wout, freq.reshape(1, -1))


class _FftPlan:
    def __init__(self, L):
        n = 2 * L
        e = int(round(math.log2(n)))
        assert 2 ** e == n and e >= 8
        self.L, self.n = L, n
        self.n1 = 2 ** (e // 2)
        self.n2 = n // self.n1
        self.h1 = self.n1 // 2
        self.kh = self.h1 + 8
        self.kv = self.h1 + 1
        n1, n2, h1, kh = self.n1, self.n2, self.h1, self.kh
        k1 = np.arange(kh)[:, None]
        a = 2 * np.pi * k1 * np.arange(h1)[None, :] / n1
        self.fa = np.concatenate([np.cos(a), -np.sin(a)], axis=0).astype(np.float32)
        a = 2 * np.pi * np.arange(n2)[:, None] * np.arange(n2)[None, :] / n2
        self.f2r, self.f2i = np.cos(a).astype(np.float32), (-np.sin(a)).astype(np.float32)
        a = 2 * np.pi * k1 * np.arange(n2)[None, :] / n
        self.twr, self.twi = np.cos(a).astype(np.float32), (-np.sin(a)).astype(np.float32)

    def fwd_tables(self):
        return [jnp.asarray(t) for t in (self.fa, self.f2r, self.f2i, self.twr, self.twi)]

    def conv_tables(self):
        n, n1, n2, h1, kh = self.n, self.n1, self.n2, self.h1, self.kh
        k1 = np.arange(kh)[None, :, None]
        a = 2 * np.pi * k1 * (np.arange(h1)[None, None, :] / n1 + np.arange(n2)[:, None, None] / n)
        fa_tw = np.concatenate([np.cos(a), -np.sin(a)], axis=1)
        f2r, f2i = self.f2r.astype(np.float64), self.f2i.astype(np.float64)
        fwd = np.block([[f2r, -f2i], [f2i, f2r]])
        inv = np.block([[f2r, f2i], [-f2i, f2r]])
        wgt = np.where(np.arange(kh) <= h1, 2.0, 0.0)
        wgt[0] = 1.0
        wgt[h1] = 1.0
        k1 = np.arange(kh)[None, None, :]
        a = 2 * np.pi * k1 * (np.arange(h1)[None, :, None] / n1 + np.arange(n2)[:, None, None] / n)
        g = np.concatenate([np.cos(a) * wgt / n, -np.sin(a) * wgt / n], axis=2)
        return [jnp.asarray(t, dtype=BF16) for t in (fa_tw, fwd, inv, g)]


def _dot_exact(a, b):
    return jnp.dot(a, b, precision=HI, preferred_element_type=F32)


def _dft_cols(plan, x_ref, fa_ref, pr_ref, pi_ref, dot):
    fa = fa_ref[...]

    def body(n2, _):
        x = x_ref[pl.ds(n2, plan.h1, stride=plan.n2), :]
        y = dot(fa, x)
        pr_ref[pl.ds(n2, plan.kh, stride=plan.n2), :] = y[:plan.kh]
        pi_ref[pl.ds(n2, plan.kh, stride=plan.n2), :] = y[plan.kh:]
        return 0

    lax.fori_loop(0, plan.n2, body, 0, unroll=4)


def _twiddled_row_dft(plan, k1, f2r, f2i, twr_ref, twi_ref, ar, ai, dot):
    twr = twr_ref[pl.ds(k1, 1), :]
    twi = twi_ref[pl.ds(k1, 1), :]
    mr = f2r * twr - f2i * twi
    mi = f2r * twi + f2i * twr
    m = jnp.concatenate([jnp.concatenate([mr, -mi], axis=1), jnp.concatenate([mi, mr], axis=1)], axis=0)
    y = dot(m, jnp.concatenate([ar, ai], axis=0))
    return y[:plan.n2], y[plan.n2:]


def _hy_spectrum_kernel(plan, kf_ref, kb_ref, fa_ref, f2r_ref, f2i_ref, twr_ref, twi_ref, c_ref, pr_ref, pi_ref):
    f2r, f2i = f2r_ref[...], f2i_ref[...]
    n2 = plan.n2
    for which, src in enumerate((kf_ref, kb_ref)):
        _dft_cols(plan, src, fa_ref, pr_ref, pi_ref, _dot_exact)

        def body(k1, _):
            r0 = pl.multiple_of(k1 * n2, n2)
            xr, xi = _twiddled_row_dft(plan, k1, f2r, f2i, twr_ref, twi_ref,
                                       pr_ref[pl.ds(r0, n2), :], pi_ref[pl.ds(r0, n2), :], _dot_exact)
            if which == 0:
                c_ref[0, pl.ds(r0, n2), :] = xr
                c_ref[1, pl.ds(r0, n2), :] = xi
            else:
                c_ref[0, pl.ds(r0, n2), :] = c_ref[0, pl.ds(r0, n2), :] + xr
                c_ref[1, pl.ds(r0, n2), :] = c_ref[1, pl.ds(r0, n2), :] - xi
            return 0

        lax.fori_loop(0, plan.kh, body, 0, unroll=2)


def _hy_spectrum(plan, kf, kb):
    rows = plan.kh * plan.n2
    nc = HY_WIDTH // LANES
    tabs = plan.fwd_tables()
    full = lambda a: pl.BlockSpec(a.shape, lambda c: (0,) * a.ndim)
    return pl.pallas_call(
        functools.partial(_hy_spectrum_kernel, plan),
        grid=(nc,),
        in_specs=[pl.BlockSpec((plan.L, LANES), lambda c: (0, c)),
                  pl.BlockSpec((plan.L, LANES), lambda c: (0, c))] + [full(t) for t in tabs],
        out_specs=pl.BlockSpec((2, rows, LANES), lambda c: (0, 0, c)),
        out_shape=jax.ShapeDtypeStruct((2, rows, HY_WIDTH), F32),
        scratch_shapes=[pltpu.VMEM((rows, LANES), F32), pltpu.VMEM((rows, LANES), F32)],
        compiler_params=_cparams(("parallel",)),
        name="hy_spectrum",
    )(kf, kb, *tabs)


def _hy_conv_kernel(plan, hv_ref, c_ref, x0g_ref, bias_ref, fa_ref, fwd_ref, inv_ref, g_ref, o_ref,
                    pr_ref, pi_ref, qr_ref, qi_ref):
    n2, kh, h1 = plan.n2, plan.kh, plan.h1
    y_ref = pr_ref

    def cols_body(j, _):
        x = hv_ref[0, pl.ds(j, h1, stride=n2), :].astype(BF16)
        y = _dotf(fa_ref[j], x)
        r0 = pl.multiple_of(j * kh, 8)
        pr_ref[pl.ds(r0, kh), :] = y[:kh]
        pi_ref[pl.ds(r0, kh), :] = y[kh:]
        return 0

    lax.fori_loop(0, n2, cols_body, 0, unroll=4)

    fwd, inv = fwd_ref[...], inv_ref[...]

    def rows_body(a, _):
        k1 = 2 * a
        x = jnp.concatenate(
            [jnp.concatenate([pr_ref[pl.ds(k1 + d, n2, stride=kh), :] for d in range(2)], axis=1),
             jnp.concatenate([pi_ref[pl.ds(k1 + d, n2, stride=kh), :] for d in range(2)], axis=1)], axis=0)
        xf = _dotf(fwd, x.astype(BF16))
        xr, xi = xf[:n2], xf[n2:]
        r0 = pl.multiple_of(k1 * n2, 2 * n2)
        cr = jnp.concatenate([c_ref[0, pl.ds(r0 + d * n2, n2), :] for d in range(2)], axis=1)
        ci = jnp.concatenate([c_ref[1, pl.ds(r0 + d * n2, n2), :] for d in range(2)], axis=1)
        z = jnp.concatenate([xr * cr - xi * ci, xr * ci + xi * cr], axis=0)
        b = _dotf(inv, z.astype(BF16))
        c = b.shape[1] // 2
        for d in range(2):
            qr_ref[pl.ds(r0 + d * n2, n2), :] = b[:n2, d * c:(d + 1) * c]
            qi_ref[pl.ds(r0 + d * n2, n2), :] = b[n2:, d * c:(d + 1) * c]
        return 0

    lax.fori_loop(0, kh // 2, rows_body, 0, unroll=4 if (kh // 2) % 4 == 0 else 2)

    def icols_body(j, _):
        b = jnp.concatenate([qr_ref[pl.ds(j, kh, stride=n2), :], qi_ref[pl.ds(j, kh, stride=n2), :]], axis=0)
        y_ref[pl.ds(j, h1, stride=n2), :] = _dotf(g_ref[j], b.astype(BF16))
        return 0

    lax.fori_loop(0, n2, icols_body, 0, unroll=4)

    bias = bias_ref[...]
    chunk = min(plan.L, 512)

    def out_body(t, _):
        r0 = pl.multiple_of(t * chunk, chunk)
        hv = hv_ref[0, pl.ds(r0, chunk), :]
        y = y_ref[pl.ds(r0, chunk), :] + hv * bias
        o_ref[0, pl.ds(r0, chunk), :] = (y * x0g_ref[0, pl.ds(r0, chunk), :].astype(F32)).astype(o_ref.dtype)
        return 0

    lax.fori_loop(0, plan.L // chunk, out_body, 0)


def _hy_conv(plan, hv, spec, x0g, hy_bias):
    B, L, _ = hv.shape
    rows = plan.kh * plan.n2
    cb = LANES
    nc = HY_WIDTH // cb
    tabs = plan.conv_tables()
    full = lambda a: pl.BlockSpec(a.shape, lambda c, b: (0,) * a.ndim)
    seq = pl.BlockSpec((1, L, cb), lambda c, b: (b, 0, c))
    seq_in = pl.BlockSpec((1, L, cb), lambda c, b: (b, 0, c), pipeline_mode=pl.Buffered(1))
    return pl.pallas_call(
        functools.partial(_hy_conv_kernel, plan),
        grid=(nc, B),
        in_specs=[seq_in,
                  pl.BlockSpec((2, rows, cb), lambda c, b: (0, 0, c), pipeline_mode=pl.Buffered(1)),
                  seq_in,
                  pl.BlockSpec((1, cb), lambda c, b: (0, c))] + [full(t) for t in tabs],
        out_specs=seq,
        out_shape=jax.ShapeDtypeStruct((B, L, HY_WIDTH), BF16),
        scratch_shapes=[pltpu.VMEM((rows, cb), F32)] * 4,
        compiler_params=_cparams(("parallel", "parallel")),
        name="hy_conv",
    )(hv, spec, x0g, hy_bias.reshape(1, HY_WIDTH), *tabs)


def _blk(index, size):
    if isinstance(index, int):
        return pl.ds(index * size, size)
    return pl.ds(pl.multiple_of(index * size, size), size)


def _chunks_per_trip(n):
    for u in (8, 4, 2):
        if n % u == 0:
            return u
    raise ValueError(f"need an even number of key chunks, got {n}")


LAG_JUMP_LIMIT = 32.0


def _flash_pipeline(n, score_fn, v_chunk_fn, m_ref, acc_ref, s_refs, mp_refs):
    u = _chunks_per_trip(n)
    m_ref[...] = jnp.full(m_ref.shape, -jnp.inf, F32)
    acc_ref[...] = jnp.zeros(acc_ref.shape, F32)

    def scores(j, slot):
        s = score_fn(j)
        s_refs[slot][...] = s
        mp_refs[slot][...] = jnp.max(s, axis=0, keepdims=True)

    def update(j, slot):
        m_old = m_ref[...]
        m_new = jnp.maximum(m_old, mp_refs[slot][...])
        alpha = jnp.exp2(m_old - m_new)
        p = jnp.exp2(s_refs[slot][...] - m_new).astype(BF16)
        acc_ref[...] = alpha * acc_ref[...] + _dotf(v_chunk_fn(j), p)
        m_ref[...] = m_new

    scores(0, 0)

    def body(jj, _):
        j = u * jj
        for d in range(u):
            scores(j + d + 1 if d + 1 < u else jnp.minimum(j + u, n - 1), (d + 1) % 2)
            update(j + d, d % 2)
        return 0

    lax.fori_loop(0, n // u, body, 0)


def _flash_lagged_tiles(nt, n, score_fn, v_chunk_fn, finish_fn, m_ref, acc_ref, done_ref, jump_ref, s_ref):
    s0 = score_fn(0, 0)
    s_ref[...] = s0
    m_ref[...] = jnp.max(s0, axis=0, keepdims=True)
    acc_ref[...] = jnp.zeros(acc_ref.shape, F32)
    done_ref[...] = jnp.ones(done_ref.shape, F32)
    jump_ref[...] = jnp.zeros(jump_ref.shape, F32)

    def step(j, s):
        m_used = m_ref[...]
        mp = jnp.max(s, axis=0, keepdims=True)
        p = jnp.exp2(s - m_used).astype(BF16)
        m_next = jnp.maximum(m_used, mp)
        alpha = jnp.exp2(m_used - m_next)
        acc_ref[...] = (acc_ref[...] + _dotf(v_chunk_fn(j), p)) * alpha
        jump_ref[...] = jnp.maximum(jump_ref[...], mp - m_used)
        m_ref[...] = m_next

    def body(i, _):
        s_cur = s_ref[...]
        for d in range(n):
            s_next = score_fn(i, d + 1) if d + 1 < n else score_fn(jnp.minimum(i + 1, nt - 1), 0)
            step(d, s_cur)
            if d == 0:
                finish_fn(jnp.maximum(i - 1, 0), done_ref[...])
            s_cur = s_next
        s_ref[...] = s_cur
        done_ref[...] = acc_ref[...]
        acc_ref[...] = jnp.zeros(acc_ref.shape, F32)
        m_ref[...] = jnp.max(s_cur, axis=0, keepdims=True)
        return 0

    lax.fori_loop(0, nt, body, 0)
    finish_fn(nt - 1, done_ref[...])


def _lagged_scratch(rows, tk, nq):
    acc = pltpu.VMEM((rows + ONES_ROWS, nq), F32)
    return [pltpu.VMEM((1, nq), F32), acc, acc, pltpu.VMEM((tk, nq), F32)]


def _exact_scratch(rows, tk, nq):
    return [pltpu.VMEM((1, nq), F32), pltpu.VMEM((rows + ONES_ROWS, nq), F32),
            pltpu.VMEM((tk, nq), F32), pltpu.VMEM((tk, nq), F32),
            pltpu.VMEM((1, nq), F32), pltpu.VMEM((1, nq), F32)]


def _with_exact_fallback(run):
    *outs, jump = run(True)
    outs = tuple(outs)
    return lax.cond(jnp.max(jump) > LAG_JUMP_LIMIT, lambda: tuple(run(False)), lambda: outs)


def _gqa_finish(acc, sg, tq):
    o = acc[:HEAD_DIM] * (1.0 / acc[HEAD_DIM:HEAD_DIM + 1])
    ot = jnp.concatenate([o[:, h * tq:(h + 1) * tq].T for h in range(GQA_GROUP)], axis=1)
    return (ot * sg.astype(F32)).astype(BF16)


def _gqa_lagged_kernel(tq, tk, q_ref, k_ref, v_ref, sg_ref, o_ref, jump_ref, m_ref, acc_ref, done_ref, s_ref):
    L = k_ref.shape[2]
    nq = GQA_GROUP * tq

    def score_fn(i, j):
        qt = q_ref[0, 0, :, _blk(i, nq)]
        return _dotf(k_ref[0, 0, _blk(j, tk), :], qt)

    def v_chunk_fn(j):
        return v_ref[0, 0, :, _blk(j, tk)]

    def finish_fn(i, acc):
        rows = _blk(i, tq)
        o_ref[0, rows, :] = _gqa_finish(acc, sg_ref[0, rows, :], tq)

    _flash_lagged_tiles(L // tq, L // tk, score_fn, v_chunk_fn, finish_fn, m_ref, acc_ref, done_ref,
                        jump_ref.at[0], s_ref)


def _gqa_exact_kernel(tk, q_ref, k_ref, v_ref, sg_ref, o_ref, m_ref, acc_ref, s0_ref, s1_ref, mp0_ref, mp1_ref):
    qt = q_ref[0, 0]
    tq = qt.shape[-1] // GQA_GROUP
    L = k_ref.shape[2]

    def score_fn(j):
        return _dotf(k_ref[0, 0, pl.ds(pl.multiple_of(j * tk, tk), tk), :], qt)

    def v_chunk_fn(j):
        return v_ref[0, 0, :, pl.ds(pl.multiple_of(j * tk, tk), tk)]

    _flash_pipeline(L // tk, score_fn, v_chunk_fn, m_ref, acc_ref, (s0_ref, s1_ref), (mp0_ref, mp1_ref))
    o_ref[0] = _gqa_finish(acc_ref[...], sg_ref[0], tq)


def _gqa_attn(qT, k, vT, sg, tq, tk):
    B, _, _, L4 = qT.shape
    L = L4 // GQA_GROUP
    gw = GQA_GROUP * HEAD_DIM
    nq = GQA_GROUP * tq
    nt = L // tq
    y_shape = jax.ShapeDtypeStruct((B, L, GQA_HEADS * HEAD_DIM), BF16)

    def run(lagged):
        if lagged:
            whole = lambda shape: pl.BlockSpec((1, 1) + shape, lambda b, g: (b, g, 0, 0))
            y_spec = pl.BlockSpec((1, L, gw), lambda b, g: (b, 0, g))
            return pl.pallas_call(
                functools.partial(_gqa_lagged_kernel, tq, tk),
                grid=(B, GQA_KV_HEADS),
                in_specs=[whole((HEAD_DIM, L4)), whole((L, HEAD_DIM)), whole((HEAD_DIM + ONES_ROWS, L)), y_spec],
                out_specs=(y_spec, pl.BlockSpec((1, 1, nq), lambda b, g: (b * GQA_KV_HEADS + g, 0, 0))),
                out_shape=(y_shape, jax.ShapeDtypeStruct((B * GQA_KV_HEADS, 1, nq), F32)),
                scratch_shapes=_lagged_scratch(HEAD_DIM, tk, nq),
                compiler_params=_cparams(("parallel", "parallel")),
                name="gqa_attn",
            )(qT, k, vT, sg)
        y_spec = pl.BlockSpec((1, tq, gw), lambda b, g, i: (b, i, g))
        return (pl.pallas_call(
            functools.partial(_gqa_exact_kernel, tk),
            grid=(B, GQA_KV_HEADS, nt),
            in_specs=[pl.BlockSpec((1, 1, HEAD_DIM, nq), lambda b, g, i: (b, g, 0, i)),
                      pl.BlockSpec((1, 1, L, HEAD_DIM), lambda b, g, i: (b, g, 0, 0)),
                      pl.BlockSpec((1, 1, HEAD_DIM + ONES_ROWS, L), lambda b, g, i: (b, g, 0, 0)),
                      y_spec],
            out_specs=y_spec,
            out_shape=y_shape,
            scratch_shapes=_exact_scratch(HEAD_DIM, tk, nq),
            compiler_params=_cparams(("parallel", "parallel", "parallel")),
            name="gqa_attn_exact",
        )(qT, k, vT, sg),)

    return _with_exact_fallback(run)[0]


def _t5_bias_kernel(t, rb_ref, o_ref):
    h = pl.program_id(0)
    kk = lax.broadcasted_iota(jnp.int32, (t, t), 0)
    qq = lax.broadcasted_iota(jnp.int32, (t, t), 1)
    nb = REL_BUCKETS // 2
    max_exact = nb // 2
    thresholds = [int(math.ceil(max_exact * (REL_MAX_DIST / max_exact) ** (j / (nb - max_exact)) - 1e-9))
                  for j in range(1, nb - max_exact)]
    for idx in range(5):
        rel = (idx - 2) * t + kk - qq
        n = jnp.abs(rel)
        large = jnp.full((t, t), max_exact, jnp.int32)
        for th in thresholds:
            large = large + (n >= th).astype(jnp.int32)
        bucket = jnp.where(rel > 0, nb, 0) + jnp.where(n < max_exact, n, large)
        val = jnp.zeros((t, t), F32)
        for b in range(REL_BUCKETS):
            val = jnp.where(bucket == b, rb_ref[h, b], val)
        o_ref[0, idx] = val * LOG2E


def _t5_bias_tiles(rel_bias, t):
    return pl.pallas_call(
        functools.partial(_t5_bias_kernel, t),
        grid=(DIFF_HEADS,),
        in_specs=[pl.BlockSpec(memory_space=pltpu.SMEM)],
        out_specs=pl.BlockSpec((1, 5, t, t), lambda h: (h, 0, 0, 0)),
        out_shape=jax.ShapeDtypeStruct((DIFF_HEADS, 5, t, t), F32),
        compiler_params=_cparams(("parallel",)),
        name="t5_bias_tiles",
    )(rel_bias.T)


def _diff_scores(t, tk, i, j, qt, k0_ref, k1_ref, bias_ref):
    tiles = tk // t
    bias = jnp.concatenate([bias_ref[0, jnp.clip(j * tiles + r - i, -2, 2) + 2] for r in range(tiles)],
                           axis=0)
    return jnp.concatenate(
        [_dotf(k0_ref[0, 0, _blk(j, tk), :], qt[:HEAD_DIM]) + bias,
         _dotf(k1_ref[0, 0, _blk(j, tk), :], qt[HEAD_DIM:]) + bias], axis=1)


def _diff_finish(acc, t, lam_init, lam_ref, g_ref, sg):
    hw = 2 * HEAD_DIM
    lp = lam_ref[...]
    lam = (jnp.exp(jnp.sum(lp[0:1] * lp[1:2], axis=1, keepdims=True))
           - jnp.exp(jnp.sum(lp[2:3] * lp[3:4], axis=1, keepdims=True)) + lam_init)
    o = acc[:hw] * (1.0 / acc[hw:hw + 1])
    o = o[:, :t] - lam * o[:, t:]
    ms = jnp.mean(o * o, axis=0, keepdims=True)
    o = o * lax.rsqrt(ms + EPS) * g_ref[...] * (1.0 - lam_init)
    return (o.T * sg.astype(F32)).astype(BF16)


def _diff_lagged_kernel(t, tk, lam_init, q_ref, k0_ref, k1_ref, v_ref, bias_ref, lam_ref, g_ref, sg_ref,
                        o_ref, jump_ref, m_ref, acc_ref, done_ref, s_ref):
    L = k0_ref.shape[2]

    def score_fn(i, j):
        return _diff_scores(t, tk, i, j, q_ref[0, 0, :, _blk(i, t)], k0_ref, k1_ref, bias_ref)

    def v_chunk_fn(j):
        return v_ref[0, 0, :, _blk(j, tk)]

    def finish_fn(i, acc):
        rows = _blk(i, t)
        o_ref[0, rows, :] = _diff_finish(acc, t, lam_init, lam_ref, g_ref, sg_ref[0, rows, :])

    _flash_lagged_tiles(L // t, L // tk, score_fn, v_chunk_fn, finish_fn, m_ref, acc_ref, done_ref,
                        jump_ref.at[0], s_ref)


def _diff_exact_kernel(t, tk, lam_init, q_ref, k0_ref, k1_ref, v_ref, bias_ref, lam_ref, g_ref, sg_ref,
                       o_ref, m_ref, acc_ref, s0_ref, s1_ref, mp0_ref, mp1_ref):
    i = pl.program_id(2)
    L = k0_ref.shape[2]
    qt = q_ref[0, 0]

    def score_fn(j):
        return _diff_scores(t, tk, i, j, qt, k0_ref, k1_ref, bias_ref)

    def v_chunk_fn(j):
        return v_ref[0, 0, :, _blk(j, tk)]

    _flash_pipeline(L // tk, score_fn, v_chunk_fn, m_ref, acc_ref, (s0_ref, s1_ref), (mp0_ref, mp1_ref))
    o_ref[0] = _diff_finish(acc_ref[...], t, lam_init, lam_ref, g_ref, sg_ref[0])


def _diff_attn(qT, k, vT, bias_tiles, lam_params, subln_g, sg, lam_init, t, tk):
    B, _, _, L = qT.shape
    hw = 2 * HEAD_DIM
    nt = L // t
    g_tab = jnp.broadcast_to(subln_g[:, None], (hw, t))
    y_shape = jax.ShapeDtypeStruct((B, L, DIFF_HEADS * hw), BF16)

    def run(lagged):
        nd = 2 if lagged else 3

        def spec(block, index):
            return pl.BlockSpec(block, (lambda b, h: index(b, h, 0)) if lagged else index)

        tq_blk = L if lagged else t
        y_spec = spec((1, tq_blk, hw), lambda b, h, i: (b, i, h))
        in_specs = [spec((1, 1, hw, tq_blk), lambda b, h, i: (b, h, 0, i)),
                    spec((1, 1, L, HEAD_DIM), lambda b, h, i: (b, 2 * h, 0, 0)),
                    spec((1, 1, L, HEAD_DIM), lambda b, h, i: (b, 2 * h + 1, 0, 0)),
                    spec((1, 1, hw + ONES_ROWS, L), lambda b, h, i: (b, h, 0, 0)),
                    spec((1, 5, t, t), lambda b, h, i: (h, 0, 0, 0)),
                    spec((4, HEAD_DIM), lambda b, h, i: (0, 0)),
                    spec((hw, t), lambda b, h, i: (0, 0)),
                    y_spec]
        args = (qT, k, k, vT, bias_tiles, lam_params, g_tab, sg)
        if lagged:
            return pl.pallas_call(
                functools.partial(_diff_lagged_kernel, t, tk, lam_init),
                grid=(B, DIFF_HEADS),
                in_specs=in_specs,
                out_specs=(y_spec, pl.BlockSpec((1, 1, 2 * t), lambda b, h: (b * DIFF_HEADS + h, 0, 0))),
                out_shape=(y_shape, jax.ShapeDtypeStruct((B * DIFF_HEADS, 1, 2 * t), F32)),
                scratch_shapes=_lagged_scratch(hw, tk, 2 * t),
                compiler_params=_cparams(("parallel",) * nd),
                name="diff_attn",
            )(*args)
        return (pl.pallas_call(
            functools.partial(_diff_exact_kernel, t, tk, lam_init),
            grid=(B, DIFF_HEADS, nt),
            in_specs=in_specs,
            out_specs=y_spec,
            out_shape=y_shape,
            scratch_shapes=_exact_scratch(hw, tk, 2 * t),
            compiler_params=_cparams(("parallel",) * nd),
            name="diff_attn_exact",
        )(*args),)

    return _with_exact_fallback(run)[0]


def _merge_kernel(final, x_ref, ng_ref, yh_ref, yg_ref, yd_ref, wm_ref, bm_ref, wh_ref, wg_ref, wd_ref,
                  wo_ref, fg_ref, o_ref):
    x = x_ref[...]
    ms = jnp.mean(x * x, axis=-1, keepdims=True)
    h = (x * lax.rsqrt(ms + EPS) * ng_ref[...]).astype(BF16)
    merged = None
    for b, (y_ref, w_ref) in enumerate(((yh_ref, wh_ref), (yg_ref, wg_ref), (yd_ref, wd_ref))):
        z = _dotf(h, wm_ref[:, b * D_MODEL:(b + 1) * D_MODEL]) + bm_ref[:, b * D_MODEL:(b + 1) * D_MODEL]
        gate = 1.0 / (1.0 + jnp.exp(-z))
        term = gate * _dotf(y_ref[...], w_ref[...])
        merged = term if merged is None else merged + term
    y = x + _dotf(merged.astype(BF16), wo_ref[...])
    if final:
        ms = jnp.mean(y * y, axis=-1, keepdims=True)
        y = y * lax.rsqrt(ms + EPS) * fg_ref[...]
    o_ref[...] = y


def _merge_out(x2, norm_g, y_hy, y_gq, y_df, w_merge, b_merge, w_hy, w_gq, w_df, w_out, final_g, final, tm):
    T = x2.shape[0]
    full = lambda a: pl.BlockSpec(a.shape, lambda i: (0,) * a.ndim)
    tok = lambda w: pl.BlockSpec((tm, w), lambda i: (i, 0))
    ng = norm_g.reshape(1, D_MODEL)
    bm = b_merge.reshape(1, -1)
    fg = final_g.reshape(1, D_MODEL)
    return pl.pallas_call(
        functools.partial(_merge_kernel, final),
        grid=(T // tm,),
        in_specs=[tok(D_MODEL), full(ng), tok(HY_WIDTH), tok(GQA_HEADS * HEAD_DIM), tok(DIFF_HEADS * 2 * HEAD_DIM),
                  full(w_merge), full(bm), full(w_hy), full(w_gq), full(w_df), full(w_out), full(fg)],
        out_specs=tok(D_MODEL),
        out_shape=jax.ShapeDtypeStruct((T, D_MODEL), F32),
        compiler_params=_cparams(("parallel",)),
        name="merge_out",
    )(x2, ng, y_hy, y_gq, y_df, w_merge, bm, w_hy, w_gq, w_df, w_out, fg)


def _rope_tables_t(L):
    rows = L // GRID_W
    row = jnp.broadcast_to(jnp.arange(rows, dtype=F32)[:, None], (rows, GRID_W)).reshape(L)
    col = jnp.broadcast_to(jnp.arange(GRID_W, dtype=F32)[None, :], (rows, GRID_W)).reshape(L)
    n_freq = HEAD_DIM // 4
    inv_freq = ROPE_THETA ** (-jnp.arange(n_freq, dtype=F32) / n_freq)
    ang = jnp.concatenate([row[:, None] * inv_freq, col[:, None] * inv_freq], axis=-1)
    return jnp.cos(ang).T, jnp.sin(ang).T


def _hyena_positions(L):
    t01 = jnp.linspace(0.0, 1.0, L, dtype=F32)[:, None]
    bands = (FILT_EMB - 1) // 2
    w = 2.0 * math.pi * jnp.arange(L, dtype=F32)[:, None] / L
    f = jnp.linspace(1e-4, bands - 1, bands, dtype=F32)[None, :]
    z = jnp.concatenate([t01, jnp.cos(f * w), -jnp.sin(f * w)], axis=-1)
    z = jnp.pad(z, ((0, 0), (0, FILT_EMB_PAD - FILT_EMB)))
    max_decay = math.log(HY_TARGET) / HY_FAST_DECAY
    min_decay = math.log(HY_TARGET) / HY_SLOW_DECAY
    deltas = jnp.linspace(min_decay, max_decay, HY_WIDTH, dtype=F32)
    window = jnp.exp(-t01 * jnp.abs(deltas)[None, :]) + HY_SHIFT
    return z, window


def _tile(L, want):
    t = min(L, want)
    assert L % t == 0
    return t


def _trunk(x, p, bias_tiles, attn_t):
    B, L, _ = x.shape
    depth = p['w_in'].shape[0]
    cos_t, sin_t = _rope_tables_t(L)
    z_pad, window = _hyena_positions(L)
    plan = _FftPlan(L)
    tm = _tile(L, 256)
    gqa_tq = _tile(L, 128)
    x2 = x.reshape(B * L, D_MODEL)
    for l in range(depth):
        (hy_u, hy_sg, gq_qT, gq_k, gq_vT, gq_sg, df_qT, df_k, df_vT, df_sg) = _in_proj(
            x2.reshape(B, L, D_MODEL), p['norm_g'][l], p['w_in_bf'][l], p['q_norm_g'][l], p['k_norm_g'][l],
            cos_t, sin_t, tm, gqa_tq)

        hv, x0g = _hy_pre(hy_u, p['hy_conv_w'][l], p['hy_conv_b'][l], hy_sg, _tile(L, 512))
        w1_pad = jnp.pad(p['hy_f_w1'][l], ((0, FILT_EMB_PAD - FILT_EMB), (0, 0)))
        kf, kb = _hy_filter(z_pad, window, w1_pad, p['hy_f_b1'][l], p['hy_f_w2'][l], p['hy_f_b2'][l],
                            p['hy_f_wout'][l], p['hy_f_freq'][l], _tile(L, 512))
        spec = _hy_spectrum(plan, kf, kb)
        y_hy = _hy_conv(plan, hv, spec, x0g, p['hy_bias'][l])

        y_gq = _gqa_attn(gq_qT, gq_k, gq_vT, gq_sg, gqa_tq, _tile(L // 2, 512))

        lam_init = 0.8 - 0.6 * math.exp(-0.3 * l)
        lam_params = jnp.stack([p['lam_q1'][l], p['lam_k1'][l], p['lam_q2'][l], p['lam_k2'][l]])
        y_df = _diff_attn(df_qT, df_k, df_vT, bias_tiles, lam_params, p['diff_subln_g'][l], df_sg, lam_init,
                          attn_t, max(attn_t, _tile(L // 2, 512)))

        x2 = _merge_out(x2, p['norm_g'][l], y_hy.reshape(B * L, -1), y_gq.reshape(B * L, -1),
                        y_df.reshape(B * L, -1), p['w_merge_bf'][l], p['b_merge'][l], p['w_branch_hy_bf'][l],
                        p['w_branch_gqa_bf'][l], p['w_branch_diff_bf'][l], p['w_out_bf'][l], p['final_g'],
                        l == depth - 1, _tile(B * L, 512))
    return x2.reshape(B, L, D_MODEL)


def kernel(x_prompt, x_sample, rel_bias, norm_g, w_in, hy_conv_w, hy_conv_b, hy_f_w1, hy_f_b1, hy_f_w2, hy_f_b2, hy_f_wout, hy_f_freq, hy_bias, q_norm_g, k_norm_g, lam_q1, lam_k1, lam_q2, lam_k2, diff_subln_g, w_branch_hy, w_branch_gqa, w_branch_diff, w_merge, b_merge, w_out, final_g):
    p = dict(norm_g=norm_g, hy_conv_w=hy_conv_w, hy_conv_b=hy_conv_b, hy_f_w1=hy_f_w1,
             hy_f_b1=hy_f_b1, hy_f_w2=hy_f_w2, hy_f_b2=hy_f_b2, hy_f_wout=hy_f_wout, hy_f_freq=hy_f_freq,
             hy_bias=hy_bias, q_norm_g=q_norm_g, k_norm_g=k_norm_g, lam_q1=lam_q1, lam_k1=lam_k1,
             lam_q2=lam_q2, lam_k2=lam_k2, diff_subln_g=diff_subln_g, b_merge=b_merge, final_g=final_g,
             w_in=w_in,
             w_in_bf=w_in.astype(BF16), w_merge_bf=w_merge.astype(BF16), w_out_bf=w_out.astype(BF16),
             w_branch_hy_bf=w_branch_hy.astype(BF16), w_branch_gqa_bf=w_branch_gqa.astype(BF16),
             w_branch_diff_bf=w_branch_diff.astype(BF16))
    outs = []
    for x in (x_prompt, x_sample):
        attn_t = _tile(x.shape[1], 256)
        bias_tiles = _t5_bias_tiles(rel_bias, attn_t)
        outs.append(_trunk(x, p, bias_tiles, attn_t))
    return tuple(outs)
```

```python
import functools
import math

import numpy as np
import jax
import jax.numpy as jnp
from jax import lax
from jax.experimental import pallas as pl
from jax.experimental.pallas import tpu as pltpu

D_MODEL = 1024
HEAD_DIM = 64
EPS = 1e-6
GRID_W = 64
ROPE_THETA = 10000.0

HY_WIDTH = 512
FILT_EMB = 33
FILT_EMB_PAD = 64
FILT_ORDER = 64
FILT_INNER = 2
HY_FAST_DECAY = 0.3
HY_SLOW_DECAY = 1.5
HY_TARGET = 1e-2
HY_SHIFT = 0.0

GQA_HEADS = 8
GQA_KV_HEADS = 2
GQA_GROUP = GQA_HEADS // GQA_KV_HEADS
DIFF_HEADS = 4
REL_BUCKETS = 32
REL_MAX_DIST = 128

C_HY_U = 0
C_HY_G = 1536
C_GQ_Q = 2048
C_GQ_K = 2560
C_GQ_V = 2688
C_GQ_G = 2816
C_DF_Q = 3328
C_DF_K = 3840
C_DF_V = 4352
C_DF_G = 4864
IN_COLS = 5376

LANES = 128
ONES_ROWS = 16
VMEM_LIMIT = 56 * 1024 * 1024
LOG2E = math.log2(math.e)
HI = lax.Precision.HIGHEST
F32 = jnp.float32
BF16 = jnp.bfloat16


def _cparams(sem):
    return pltpu.CompilerParams(dimension_semantics=sem, vmem_limit_bytes=VMEM_LIMIT)


def _silu(x):
    return x * (1.0 / (1.0 + jnp.exp(-x)))


def _dotf(a, b):
    return jnp.dot(a, b, preferred_element_type=F32)


def _norm_rope_t(xt, g_tab, cos, sin):
    ms = jnp.mean(xt * xt, axis=1, keepdims=True)
    xn = xt * lax.rsqrt(ms + EPS) * g_tab[None]
    half = HEAD_DIM // 2
    x1, x2 = xn[:, :half, :], xn[:, half:, :]
    c, s = cos[None], sin[None]
    return jnp.concatenate([x1 * c - x2 * s, x2 * c + x1 * s], axis=1)


def _in_proj_kernel(tq, x_ref, ng_ref, w_ref, qg_ref, kg_ref, cos_ref, sin_ref,
                    hyu_ref, hysg_ref, gqq_ref, gqk_ref, gqv_ref, gqsg_ref,
                    dfq_ref, dfk_ref, dfv_ref, dfsg_ref):
    x = x_ref[0]
    tm = x.shape[0]
    ms = jnp.mean(x * x, axis=-1, keepdims=True)
    h = (x * lax.rsqrt(ms + EPS) * ng_ref[...]).astype(BF16)

    def proj(lo, hi):
        return _dotf(h, w_ref[:, lo:hi])

    hyu_ref[0] = proj(C_HY_U, C_HY_G)
    hysg_ref[0] = _silu(proj(C_HY_G, C_GQ_Q)).astype(BF16)

    cos, sin = cos_ref[...], sin_ref[...]
    scale = HEAD_DIM ** -0.5
    qt = proj(C_GQ_Q, C_GQ_K).T.reshape(GQA_HEADS, HEAD_DIM, tm)
    qn = (_norm_rope_t(qt, qg_ref[...], cos, sin) * (scale * LOG2E)).astype(BF16)
    for g in range(GQA_KV_HEADS):
        for a in range(tm // tq):
            for hh in range(GQA_GROUP):
                c0 = (a * GQA_GROUP + hh) * tq
                gqq_ref[0, g, :, c0:c0 + tq] = qn[g * GQA_GROUP + hh][:, a * tq:(a + 1) * tq]
    kt = proj(C_GQ_K, C_GQ_V).T.reshape(GQA_KV_HEADS, HEAD_DIM, tm)
    kt = _norm_rope_t(kt, kg_ref[...], cos, sin)
    for g in range(GQA_KV_HEADS):
        gqk_ref[0, g] = kt[g].T.astype(BF16)
    ones_rows = (lax.broadcasted_iota(jnp.int32, (ONES_ROWS, tm), 0) == 0).astype(BF16)
    vt = proj(C_GQ_V, C_GQ_G).T.astype(BF16)
    for g in range(GQA_KV_HEADS):
        gqv_ref[0, g, :HEAD_DIM, :] = vt[g * HEAD_DIM:(g + 1) * HEAD_DIM]
        gqv_ref[0, g, HEAD_DIM:, :] = ones_rows
    gqsg_ref[0] = _silu(proj(C_GQ_G, C_DF_Q)).astype(BF16)

    dfq_ref[0] = (proj(C_DF_Q, C_DF_K) * (scale * LOG2E)).T.reshape(DIFF_HEADS, 2 * HEAD_DIM, tm).astype(BF16)
    dk = proj(C_DF_K, C_DF_V).astype(BF16)
    for j in range(2 * DIFF_HEADS):
        dfk_ref[0, j] = dk[:, j * HEAD_DIM:(j + 1) * HEAD_DIM]
    dvt = proj(C_DF_V, C_DF_G).T.astype(BF16)
    for hh in range(DIFF_HEADS):
        dfv_ref[0, hh, :2 * HEAD_DIM, :] = dvt[hh * 2 * HEAD_DIM:(hh + 1) * 2 * HEAD_DIM]
        dfv_ref[0, hh, 2 * HEAD_DIM:, :] = ones_rows
    dfsg_ref[0] = _silu(proj(C_DF_G, IN_COLS)).astype(BF16)


def _in_proj(x, norm_g, w_in_bf, q_g, k_g, cos_t, sin_t, tm, tq):
    B, L, _ = x.shape
    assert tm % tq == 0
    nt = L // tm
    qg_tab = jnp.broadcast_to(q_g[:, None], (HEAD_DIM, tm))
    kg_tab = jnp.broadcast_to(k_g[:, None], (HEAD_DIM, tm))
    full = lambda shape: pl.BlockSpec(shape, lambda b, i: (0,) * len(shape))
    out_shapes = (
        jax.ShapeDtypeStruct((B, L, 3 * HY_WIDTH), F32),
        jax.ShapeDtypeStruct((B, L, HY_WIDTH), BF16),
        jax.ShapeDtypeStruct((B, GQA_KV_HEADS, HEAD_DIM, GQA_GROUP * L), BF16),
        jax.ShapeDtypeStruct((B, GQA_KV_HEADS, L, HEAD_DIM), BF16),
        jax.ShapeDtypeStruct((B, GQA_KV_HEADS, HEAD_DIM + ONES_ROWS, L), BF16),
        jax.ShapeDtypeStruct((B, L, GQA_HEADS * HEAD_DIM), BF16),
        jax.ShapeDtypeStruct((B, DIFF_HEADS, 2 * HEAD_DIM, L), BF16),
        jax.ShapeDtypeStruct((B, 2 * DIFF_HEADS, L, HEAD_DIM), BF16),
        jax.ShapeDtypeStruct((B, DIFF_HEADS, 2 * HEAD_DIM + ONES_ROWS, L), BF16),
        jax.ShapeDtypeStruct((B, L, DIFF_HEADS * 2 * HEAD_DIM), BF16),
    )
    tok = lambda w: pl.BlockSpec((1, tm, w), lambda b, i: (b, i, 0))
    tr = lambda h, d: pl.BlockSpec((1, h, d, tm), lambda b, i: (b, 0, 0, i))
    rows = lambda h: pl.BlockSpec((1, h, tm, HEAD_DIM), lambda b, i: (b, 0, i, 0))
    return pl.pallas_call(
        functools.partial(_in_proj_kernel, tq),
        grid=(B, nt),
        in_specs=[
            tok(D_MODEL),
            full((1, D_MODEL)),
            full((D_MODEL, IN_COLS)),
            full((HEAD_DIM, tm)),
            full((HEAD_DIM, tm)),
            pl.BlockSpec((HEAD_DIM // 2, tm), lambda b, i: (0, i)),
            pl.BlockSpec((HEAD_DIM // 2, tm), lambda b, i: (0, i)),
        ],
        out_specs=(
            tok(3 * HY_WIDTH), tok(HY_WIDTH),
            pl.BlockSpec((1, GQA_KV_HEADS, HEAD_DIM, GQA_GROUP * tm), lambda b, i: (b, 0, 0, i)),
            rows(GQA_KV_HEADS), tr(GQA_KV_HEADS, HEAD_DIM + ONES_ROWS),
            tok(GQA_HEADS * HEAD_DIM),
            tr(DIFF_HEADS, 2 * HEAD_DIM), rows(2 * DIFF_HEADS), tr(DIFF_HEADS, 2 * HEAD_DIM + ONES_ROWS),
            tok(DIFF_HEADS * 2 * HEAD_DIM),
        ),
        out_shape=out_shapes,
        compiler_params=_cparams(("parallel", "parallel")),
        name="in_proj",
    )(x, norm_g.reshape(1, D_MODEL), w_in_bf, qg_tab, kg_tab, cos_t, sin_t)


def _hy_pre_kernel(u_ref, prev_ref, next_ref, w_ref, b_ref, sg_ref, hv_ref, x0g_ref):
    i = pl.program_id(1)
    nt = pl.num_programs(1)
    u = u_ref[0]
    tl = u.shape[0]
    prev_row = jnp.where(i > 0, prev_ref[0, 0, 7:8, :], 0.0)
    next_row = jnp.where(i < nt - 1, next_ref[0, 0, 0:1, :], 0.0)
    row = lax.broadcasted_iota(jnp.int32, u.shape, 0)
    up = jnp.where(row == 0, prev_row, pltpu.roll(u, 1, 0))
    dn = jnp.where(row == tl - 1, next_row, pltpu.roll(u, tl - 1, 0))
    w = w_ref[...]
    hy = up * w[0:1] + u * w[1:2] + dn * w[2:3] + b_ref[...]
    x0 = hy[:, :HY_WIDTH]
    x1 = hy[:, HY_WIDTH:2 * HY_WIDTH]
    hv = hy[:, 2 * HY_WIDTH:]
    hv_ref[0] = hv * x1
    x0g_ref[0] = (x0 * sg_ref[0].astype(F32)).astype(x0g_ref.dtype)


def _hy_pre(hy_u, conv_w, conv_b, hy_sg, tl):
    B, L, W3 = hy_u.shape
    nt = L // tl
    g8 = tl // 8
    u4 = hy_u.reshape(B, L // 8, 8, W3)
    return pl.pallas_call(
        _hy_pre_kernel,
        grid=(B, nt),
        in_specs=[
            pl.BlockSpec((1, tl, W3), lambda b, i: (b, i, 0)),
            pl.BlockSpec((1, 1, 8, W3), lambda b, i: (b, jnp.maximum(i * g8 - 1, 0), 0, 0)),
            pl.BlockSpec((1, 1, 8, W3), lambda b, i: (b, jnp.minimum((i + 1) * g8, L // 8 - 1), 0, 0)),
            pl.BlockSpec((3, W3), lambda b, i: (0, 0)),
            pl.BlockSpec((1, W3), lambda b, i: (0, 0)),
            pl.BlockSpec((1, tl, HY_WIDTH), lambda b, i: (b, i, 0)),
        ],
        out_specs=(
            pl.BlockSpec((1, tl, HY_WIDTH), lambda b, i: (b, i, 0)),
            pl.BlockSpec((1, tl, HY_WIDTH), lambda b, i: (b, i, 0)),
        ),
        out_shape=(
            jax.ShapeDtypeStruct((B, L, HY_WIDTH), F32),
            jax.ShapeDtypeStruct((B, L, HY_WIDTH), BF16),
        ),
        compiler_params=_cparams(("parallel", "parallel")),
        name="hy_pre",
    )(hy_u, u4, u4, conv_w, conv_b.reshape(1, W3), hy_sg)


def _hy_filter_kernel(z_ref, win_ref, w1_ref, b1_ref, w2_ref, b2_ref, wout_ref, freq_ref, kf_ref, kb_ref):
    i = pl.program_id(0)
    freq = freq_ref[...]
    a = jnp.sin(freq * (jnp.dot(z_ref[...], w1_ref[...], precision=HI, preferred_element_type=F32) + b1_ref[...]))
    for j in range(FILT_INNER):
        a = jnp.sin(freq * (jnp.dot(a, w2_ref[j], precision=HI, preferred_element_type=F32) + b2_ref[j:j + 1, :]))
    hf = jnp.dot(a, wout_ref[...], precision=HI, preferred_element_type=F32)
    win = win_ref[...]
    kf_ref[...] = hf[:, :HY_WIDTH] * win
    row = lax.broadcasted_iota(jnp.int32, win.shape, 0)
    kb_ref[...] = jnp.where((row == 0) & (i == 0), 0.0, hf[:, HY_WIDTH:] * win)


def _hy_filter(z_pad, window, w1_pad, b1, w2, b2, wout, freq, tl):
    L = z_pad.shape[0]
    full = lambda shape: pl.BlockSpec(shape, lambda i: (0,) * len(shape))
    return pl.pallas_call(
        _hy_filter_kernel,
        grid=(L // tl,),
        in_specs=[
            pl.BlockSpec((tl, FILT_EMB_PAD), lambda i: (i, 0)),
            pl.BlockSpec((tl, HY_WIDTH), lambda i: (i, 0)),
            full((FILT_EMB_PAD, FILT_ORDER)),
            full((1, FILT_ORDER)),
            full((FILT_INNER, FILT_ORDER, FILT_ORDER)),
            full((FILT_INNER, FILT_ORDER)),
            full((FILT_ORDER, 2 * HY_WIDTH)),
            full((1, FILT_ORDER)),
        ],
        out_specs=(
            pl.BlockSpec((tl, HY_WIDTH), lambda i: (i, 0)),
            pl.BlockSpec((tl, HY_WIDTH), lambda i: (i, 0)),
        ),
        out_shape=(
            jax.ShapeDtypeStruct((L, HY_WIDTH), F32),
            jax.ShapeDtypeStruct((L, HY_WIDTH), F32),
        ),
        compiler_params=_cparams(("parallel",)),
        name="hy_filter",
    )(z_pad, window, w1_pad, b1.reshape(1, -1), w2, b2, wout, freq.reshape(1, -1))


class _FftPlan:
    def __init__(self, L):
        n = 2 * L
        e = int(round(math.log2(n)))
        assert 2 ** e == n and e >= 8
        self.L, self.n = L, n
        self.n1 = 2 ** (e // 2)
        self.n2 = n // self.n1
        self.h1 = self.n1 // 2
        self.kh = self.h1 + 8
        self.kv = self.h1 + 1
        n1, n2, h1, kh = self.n1, self.n2, self.h1, self.kh
        k1 = np.arange(kh)[:, None]
        a = 2 * np.pi * k1 * np.arange(h1)[None, :] / n1
        self.fa = np.concatenate([np.cos(a), -np.sin(a)], axis=0).astype(np.float32)
        a = 2 * np.pi * np.arange(n2)[:, None] * np.arange(n2)[None, :] / n2
        self.f2r, self.f2i = np.cos(a).astype(np.float32), (-np.sin(a)).astype(np.float32)
        a = 2 * np.pi * k1 * np.arange(n2)[None, :] / n
        self.twr, self.twi = np.cos(a).astype(np.float32), (-np.sin(a)).astype(np.float32)

    def fwd_tables(self):
        return [jnp.asarray(t) for t in (self.fa, self.f2r, self.f2i, self.twr, self.twi)]

    def conv_tables(self):
        n, n1, n2, h1, kh = self.n, self.n1, self.n2, self.h1, self.kh
        k1 = np.arange(kh)[None, :, None]
        a = 2 * np.pi * k1 * (np.arange(h1)[None, None, :] / n1 + np.arange(n2)[:, None, None] / n)
        fa_tw = np.concatenate([np.cos(a), -np.sin(a)], axis=1)
        f2r, f2i = self.f2r.astype(np.float64), self.f2i.astype(np.float64)
        fwd = np.block([[f2r, -f2i], [f2i, f2r]])
        inv = np.block([[f2r, f2i], [-f2i, f2r]])
        wgt = np.where(np.arange(kh) <= h1, 2.0, 0.0)
        wgt[0] = 1.0
        wgt[h1] = 1.0
        k1 = np.arange(kh)[None, None, :]
        a = 2 * np.pi * k1 * (np.arange(h1)[None, :, None] / n1 + np.arange(n2)[:, None, None] / n)
        g = np.concatenate([np.cos(a) * wgt / n, -np.sin(a) * wgt / n], axis=2)
        return [jnp.asarray(t, dtype=BF16) for t in (fa_tw, fwd, inv, g)]


def _dot_exact(a, b):
    ah, bh = a.astype(BF16), b.astype(BF16)
    al, bl = (a - ah.astype(F32)).astype(BF16), (b - bh.astype(F32)).astype(BF16)
    return _dotf(ah, bh) + (_dotf(ah, bl) + _dotf(al, bh))


def _dft_cols(plan, x_ref, fa_ref, pr_ref, pi_ref, dot):
    fa = fa_ref[...]

    def body(n2, _):
        x = x_ref[pl.ds(n2, plan.h1, stride=plan.n2), :]
        y = dot(fa, x)
        pr_ref[pl.ds(n2, plan.kh, stride=plan.n2), :] = y[:plan.kh]
        pi_ref[pl.ds(n2, plan.kh, stride=plan.n2), :] = y[plan.kh:]
        return 0

    lax.fori_loop(0, plan.n2, body, 0, unroll=4)


def _twiddled_row_dft(plan, k1, f2r, f2i, twr_ref, twi_ref, ar, ai, dot):
    twr = twr_ref[pl.ds(k1, 1), :]
    twi = twi_ref[pl.ds(k1, 1), :]
    mr = f2r * twr - f2i * twi
    mi = f2r * twi + f2i * twr
    m = jnp.concatenate([jnp.concatenate([mr, -mi], axis=1), jnp.concatenate([mi, mr], axis=1)], axis=0)
    y = dot(m, jnp.concatenate([ar, ai], axis=0))
    return y[:plan.n2], y[plan.n2:]


def _hy_spectrum_kernel(plan, kf_ref, kb_ref, fa_ref, f2r_ref, f2i_ref, twr_ref, twi_ref, c_ref, pr_ref, pi_ref):
    f2r, f2i = f2r_ref[...], f2i_ref[...]
    n2 = plan.n2
    for which, src in enumerate((kf_ref, kb_ref)):
        _dft_cols(plan, src, fa_ref, pr_ref, pi_ref, _dot_exact)

        def body(k1, _):
            r0 = pl.multiple_of(k1 * n2, n2)
            xr, xi = _twiddled_row_dft(plan, k1, f2r, f2i, twr_ref, twi_ref,
                                       pr_ref[pl.ds(r0, n2), :], pi_ref[pl.ds(r0, n2), :], _dot_exact)
            if which == 0:
                c_ref[0, pl.ds(r0, n2), :] = xr
                c_ref[1, pl.ds(r0, n2), :] = xi
            else:
                c_ref[0, pl.ds(r0, n2), :] = c_ref[0, pl.ds(r0, n2), :] + xr
                c_ref[1, pl.ds(r0, n2), :] = c_ref[1, pl.ds(r0, n2), :] - xi
            return 0

        lax.fori_loop(0, plan.kh, body, 0, unroll=2)


def _hy_spectrum(plan, kf, kb):
    rows = plan.kh * plan.n2
    nc = HY_WIDTH // LANES
    tabs = plan.fwd_tables()
    full = lambda a: pl.BlockSpec(a.shape, lambda c: (0,) * a.ndim)
    return pl.pallas_call(
        functools.partial(_hy_spectrum_kernel, plan),
        grid=(nc,),
        in_specs=[pl.BlockSpec((plan.L, LANES), lambda c: (0, c)),
                  pl.BlockSpec((plan.L, LANES), lambda c: (0, c))] + [full(t) for t in tabs],
        out_specs=pl.BlockSpec((2, rows, LANES), lambda c: (0, 0, c)),
        out_shape=jax.ShapeDtypeStruct((2, rows, HY_WIDTH), F32),
        scratch_shapes=[pltpu.VMEM((rows, LANES), F32), pltpu.VMEM((rows, LANES), F32)],
        compiler_params=_cparams(("parallel",)),
        name="hy_spectrum",
    )(kf, kb, *tabs)


SUBLANES = 8


def _hy_conv_kernel(plan, hv_ref, c_ref, x0g_ref, bias_ref, fa_ref, fwd_ref, inv_ref, g_ref, o_ref,
                    pr_ref, pi_ref, qr_ref, qi_ref, y_ref):
    n2, kh, h1 = plan.n2, plan.kh, plan.h1
    eight = lambda a: pl.ds(pl.multiple_of(a * SUBLANES, SUBLANES), SUBLANES)

    def cols_body(a, _):
        slab = jnp.swapaxes(hv_ref[0, :, eight(a), :], 0, 1).astype(BF16)
        for s in range(SUBLANES):
            j = a * SUBLANES + s
            y = _dotf(fa_ref[j], slab[s])
            pr_ref[j] = y[:kh]
            pi_ref[j] = y[kh:]
        return 0

    lax.fori_loop(0, n2 // SUBLANES, cols_body, 0)

    fwd, inv = fwd_ref[...], inv_ref[...]

    def rows_body(c, _):
        xr8 = jnp.swapaxes(pr_ref[:, eight(c), :], 0, 1)
        xi8 = jnp.swapaxes(pi_ref[:, eight(c), :], 0, 1)
        for d in range(0, SUBLANES, 2):
            k1 = c * SUBLANES + d
            x = jnp.concatenate([jnp.concatenate([xr8[d], xr8[d + 1]], axis=1),
                                 jnp.concatenate([xi8[d], xi8[d + 1]], axis=1)], axis=0)
            xf = _dotf(fwd, x.astype(BF16))
            xr, xi = xf[:n2], xf[n2:]
            cr = jnp.concatenate([c_ref[0, k1], c_ref[0, k1 + 1]], axis=1)
            ci = jnp.concatenate([c_ref[1, k1], c_ref[1, k1 + 1]], axis=1)
            z = jnp.concatenate([xr * cr - xi * ci, xr * ci + xi * cr], axis=0)
            b = _dotf(inv, z.astype(BF16))
            w = b.shape[1] // 2
            for e in range(2):
                qr_ref[k1 + e] = b[:n2, e * w:(e + 1) * w]
                qi_ref[k1 + e] = b[n2:, e * w:(e + 1) * w]
        return 0

    lax.fori_loop(0, kh // SUBLANES, rows_body, 0)

    def icols_body(a, _):
        br8 = jnp.swapaxes(qr_ref[:, eight(a), :], 0, 1)
        bi8 = jnp.swapaxes(qi_ref[:, eight(a), :], 0, 1)
        ys = [_dotf(g_ref[a * SUBLANES + s], jnp.concatenate([br8[s], bi8[s]], axis=0).astype(BF16))
              for s in range(SUBLANES)]
        y_ref[:, eight(a), :] = jnp.swapaxes(jnp.stack(ys, axis=0), 0, 1)
        return 0

    lax.fori_loop(0, n2 // SUBLANES, icols_body, 0)

    bias = bias_ref[...]
    rows = max(1, 512 // n2)

    def out_body(t, _):
        r = pl.ds(t * rows, rows)
        y = y_ref[r] + hv_ref[0, r] * bias
        o_ref[0, r] = (y * x0g_ref[0, r].astype(F32)).astype(o_ref.dtype)
        return 0

    lax.fori_loop(0, h1 // rows, out_body, 0)


def _hy_conv(plan, hv, spec, x0g, hy_bias):
    B, L, _ = hv.shape
    n2, kh, h1 = plan.n2, plan.kh, plan.h1
    cb = LANES
    nc = HY_WIDTH // cb
    tabs = plan.conv_tables()
    full = lambda a: pl.BlockSpec(a.shape, lambda c, b: (0,) * a.ndim)
    view = lambda a: a.reshape(B, h1, n2, HY_WIDTH)
    seq = pl.BlockSpec((1, h1, n2, cb), lambda c, b: (b, 0, 0, c))
    seq_in = pl.BlockSpec((1, h1, n2, cb), lambda c, b: (b, 0, 0, c), pipeline_mode=pl.Buffered(1))
    out = pl.pallas_call(
        functools.partial(_hy_conv_kernel, plan),
        grid=(nc, B),
        in_specs=[seq_in,
                  pl.BlockSpec((2, kh, n2, cb), lambda c, b: (0, 0, 0, c), pipeline_mode=pl.Buffered(1)),
                  seq_in,
                  pl.BlockSpec((1, cb), lambda c, b: (0, c))] + [full(t) for t in tabs],
        out_specs=seq,
        out_shape=jax.ShapeDtypeStruct((B, h1, n2, HY_WIDTH), BF16),
        scratch_shapes=[pltpu.VMEM((n2, kh, cb), F32), pltpu.VMEM((n2, kh, cb), F32),
                        pltpu.VMEM((kh, n2, cb), F32), pltpu.VMEM((kh, n2, cb), F32),
                        pltpu.VMEM((h1, n2, cb), F32)],
        compiler_params=_cparams(("parallel", "parallel")),
        name="hy_conv",
    )(view(hv), spec.reshape(2, kh, n2, HY_WIDTH), view(x0g), hy_bias.reshape(1, HY_WIDTH), *tabs)
    return out.reshape(B, L, HY_WIDTH)


def _blk(index, size):
    if isinstance(index, int):
        return pl.ds(index * size, size)
    return pl.ds(pl.multiple_of(index * size, size), size)


def _chunks_per_trip(n):
    for u in (8, 4, 2):
        if n % u == 0:
            return u
    raise ValueError(f"need an even number of key chunks, got {n}")


LAG_JUMP_LIMIT = 32.0


def _flash_pipeline(n, score_fn, v_chunk_fn, m_ref, acc_ref, s_refs, mp_refs):
    u = _chunks_per_trip(n)
    m_ref[...] = jnp.full(m_ref.shape, -jnp.inf, F32)
    acc_ref[...] = jnp.zeros(acc_ref.shape, F32)

    def scores(j, slot):
        s = score_fn(j)
        s_refs[slot][...] = s
        mp_refs[slot][...] = jnp.max(s, axis=0, keepdims=True)

    def update(j, slot):
        m_old = m_ref[...]
        m_new = jnp.maximum(m_old, mp_refs[slot][...])
        alpha = jnp.exp2(m_old - m_new)
        p = jnp.exp2(s_refs[slot][...] - m_new).astype(BF16)
        acc_ref[...] = alpha * acc_ref[...] + _dotf(v_chunk_fn(j), p)
        m_ref[...] = m_new

    scores(0, 0)

    def body(jj, _):
        j = u * jj
        for d in range(u):
            scores(j + d + 1 if d + 1 < u else jnp.minimum(j + u, n - 1), (d + 1) % 2)
            update(j + d, d % 2)
        return 0

    lax.fori_loop(0, n // u, body, 0)


def _flash_lagged_tiles(nt, n, score_fn, v_chunk_fn, finish_fn, m_ref, acc_ref, done_ref, jump_ref, s_ref):
    s0 = score_fn(0, 0)
    s_ref[...] = s0
    m_ref[...] = jnp.max(s0, axis=0, keepdims=True)
    acc_ref[...] = jnp.zeros(acc_ref.shape, F32)
    done_ref[...] = jnp.ones(done_ref.shape, F32)
    jump_ref[...] = jnp.zeros(jump_ref.shape, F32)

    def step(j, s):
        m_used = m_ref[...]
        p = jnp.exp2(s - m_used).astype(BF16)
        rise = jnp.log2(jnp.max(p, axis=0, keepdims=True).astype(F32))
        up = jnp.maximum(rise, 0.0)
        acc_ref[...] = (acc_ref[...] + _dotf(v_chunk_fn(j), p)) * jnp.exp2(-up)
        jump_ref[...] = jnp.maximum(jump_ref[...], rise)
        m_ref[...] = m_used + up

    def body(i, _):
        s_cur = s_ref[...]
        for d in range(n):
            s_next = score_fn(i, d + 1) if d + 1 < n else score_fn(jnp.minimum(i + 1, nt - 1), 0)
            step(d, s_cur)
            if d == 0:
                finish_fn(jnp.maximum(i - 1, 0), done_ref[...])
            s_cur = s_next
        s_ref[...] = s_cur
        done_ref[...] = acc_ref[...]
        acc_ref[...] = jnp.zeros(acc_ref.shape, F32)
        m_ref[...] = jnp.max(s_cur, axis=0, keepdims=True)
        return 0

    lax.fori_loop(0, nt, body, 0)
    finish_fn(nt - 1, done_ref[...])


def _lagged_scratch(rows, tk, nq):
    acc = pltpu.VMEM((rows + ONES_ROWS, nq), F32)
    return [pltpu.VMEM((1, nq), F32), acc, acc, pltpu.VMEM((tk, nq), F32)]


def _exact_scratch(rows, tk, nq):
    return [pltpu.VMEM((1, nq), F32), pltpu.VMEM((rows + ONES_ROWS, nq), F32),
            pltpu.VMEM((tk, nq), F32), pltpu.VMEM((tk, nq), F32),
            pltpu.VMEM((1, nq), F32), pltpu.VMEM((1, nq), F32)]


def _with_exact_fallback(run):
    *outs, jump = run(True)
    outs = tuple(outs)
    return lax.cond(jnp.max(jump) > LAG_JUMP_LIMIT, lambda: tuple(run(False)), lambda: outs)


def _gqa_finish(acc, sg, tq):
    o = acc[:HEAD_DIM] * (1.0 / acc[HEAD_DIM:HEAD_DIM + 1])
    ot = jnp.concatenate([o[:, h * tq:(h + 1) * tq].T for h in range(GQA_GROUP)], axis=1)
    return (ot * sg.astype(F32)).astype(BF16)


def _gqa_lagged_kernel(tq, tk, q_ref, k_ref, v_ref, sg_ref, o_ref, jump_ref, m_ref, acc_ref, done_ref, s_ref):
    L = k_ref.shape[2]
    nq = GQA_GROUP * tq

    def score_fn(i, j):
        qt = q_ref[0, 0, :, _blk(i, nq)]
        return _dotf(k_ref[0, 0, _blk(j, tk), :], qt)

    def v_chunk_fn(j):
        return v_ref[0, 0, :, _blk(j, tk)]

    def finish_fn(i, acc):
        rows = _blk(i, tq)
        o_ref[0, rows, :] = _gqa_finish(acc, sg_ref[0, rows, :], tq)

    _flash_lagged_tiles(L // tq, L // tk, score_fn, v_chunk_fn, finish_fn, m_ref, acc_ref, done_ref,
                        jump_ref.at[0], s_ref)


def _gqa_exact_kernel(tk, q_ref, k_ref, v_ref, sg_ref, o_ref, m_ref, acc_ref, s0_ref, s1_ref, mp0_ref, mp1_ref):
    qt = q_ref[0, 0]
    tq = qt.shape[-1] // GQA_GROUP
    L = k_ref.shape[2]

    def score_fn(j):
        return _dotf(k_ref[0, 0, pl.ds(pl.multiple_of(j * tk, tk), tk), :], qt)

    def v_chunk_fn(j):
        return v_ref[0, 0, :, pl.ds(pl.multiple_of(j * tk, tk), tk)]

    _flash_pipeline(L // tk, score_fn, v_chunk_fn, m_ref, acc_ref, (s0_ref, s1_ref), (mp0_ref, mp1_ref))
    o_ref[0] = _gqa_finish(acc_ref[...], sg_ref[0], tq)


def _gqa_attn(qT, k, vT, sg, tq, tk):
    B, _, _, L4 = qT.shape
    L = L4 // GQA_GROUP
    gw = GQA_GROUP * HEAD_DIM
    nq = GQA_GROUP * tq
    nt = L // tq
    y_shape = jax.ShapeDtypeStruct((B, L, GQA_HEADS * HEAD_DIM), BF16)

    def run(lagged):
        if lagged:
            whole = lambda shape: pl.BlockSpec((1, 1) + shape, lambda b, g: (b, g, 0, 0))
            y_spec = pl.BlockSpec((1, L, gw), lambda b, g: (b, 0, g))
            return pl.pallas_call(
                functools.partial(_gqa_lagged_kernel, tq, tk),
                grid=(B, GQA_KV_HEADS),
                in_specs=[whole((HEAD_DIM, L4)), whole((L, HEAD_DIM)), whole((HEAD_DIM + ONES_ROWS, L)), y_spec],
                out_specs=(y_spec, pl.BlockSpec((1, 1, nq), lambda b, g: (b * GQA_KV_HEADS + g, 0, 0))),
                out_shape=(y_shape, jax.ShapeDtypeStruct((B * GQA_KV_HEADS, 1, nq), F32)),
                scratch_shapes=_lagged_scratch(HEAD_DIM, tk, nq),
                compiler_params=_cparams(("parallel", "parallel")),
                name="gqa_attn",
            )(qT, k, vT, sg)
        y_spec = pl.BlockSpec((1, tq, gw), lambda b, g, i: (b, i, g))
        return (pl.pallas_call(
            functools.partial(_gqa_exact_kernel, tk),
            grid=(B, GQA_KV_HEADS, nt),
            in_specs=[pl.BlockSpec((1, 1, HEAD_DIM, nq), lambda b, g, i: (b, g, 0, i)),
                      pl.BlockSpec((1, 1, L, HEAD_DIM), lambda b, g, i: (b, g, 0, 0)),
                      pl.BlockSpec((1, 1, HEAD_DIM + ONES_ROWS, L), lambda b, g, i: (b, g, 0, 0)),
                      y_spec],
            out_specs=y_spec,
            out_shape=y_shape,
            scratch_shapes=_exact_scratch(HEAD_DIM, tk, nq),
            compiler_params=_cparams(("parallel", "parallel", "parallel")),
            name="gqa_attn_exact",
        )(qT, k, vT, sg),)

    return _with_exact_fallback(run)[0]


def _t5_bias_kernel(t, rb_ref, o_ref):
    h = pl.program_id(0)
    kk = lax.broadcasted_iota(jnp.int32, (t, t), 0)
    qq = lax.broadcasted_iota(jnp.int32, (t, t), 1)
    nb = REL_BUCKETS // 2
    max_exact = nb // 2
    thresholds = [int(math.ceil(max_exact * (REL_MAX_DIST / max_exact) ** (j / (nb - max_exact)) - 1e-9))
                  for j in range(1, nb - max_exact)]
    for idx in range(5):
        rel = (idx - 2) * t + kk - qq
        n = jnp.abs(rel)
        large = jnp.full((t, t), max_exact, jnp.int32)
        for th in thresholds:
            large = large + (n >= th).astype(jnp.int32)
        bucket = jnp.where(rel > 0, nb, 0) + jnp.where(n < max_exact, n, large)
        val = jnp.zeros((t, t), F32)
        for b in range(REL_BUCKETS):
            val = jnp.where(bucket == b, rb_ref[h, b], val)
        o_ref[0, idx] = val * LOG2E


def _t5_bias_tiles(rel_bias, t):
    return pl.pallas_call(
        functools.partial(_t5_bias_kernel, t),
        grid=(DIFF_HEADS,),
        in_specs=[pl.BlockSpec(memory_space=pltpu.SMEM)],
        out_specs=pl.BlockSpec((1, 5, t, t), lambda h: (h, 0, 0, 0)),
        out_shape=jax.ShapeDtypeStruct((DIFF_HEADS, 5, t, t), F32),
        compiler_params=_cparams(("parallel",)),
        name="t5_bias_tiles",
    )(rel_bias.T)


def _diff_scores(t, tk, i, j, qt, k0_ref, k1_ref, bias_ref):
    tiles = tk // t
    bias = jnp.concatenate([bias_ref[0, jnp.clip(j * tiles + r - i, -2, 2) + 2] for r in range(tiles)],
                           axis=0)
    return jnp.concatenate(
        [_dotf(k0_ref[0, 0, _blk(j, tk), :], qt[:HEAD_DIM]) + bias,
         _dotf(k1_ref[0, 0, _blk(j, tk), :], qt[HEAD_DIM:]) + bias], axis=1)


def _diff_finish(acc, t, lam_init, lam_ref, g_ref, sg):
    hw = 2 * HEAD_DIM
    lp = lam_ref[...]
    lam = (jnp.exp(jnp.sum(lp[0:1] * lp[1:2], axis=1, keepdims=True))
           - jnp.exp(jnp.sum(lp[2:3] * lp[3:4], axis=1, keepdims=True)) + lam_init)
    o = acc[:hw] * (1.0 / acc[hw:hw + 1])
    o = o[:, :t] - lam * o[:, t:]
    ms = jnp.mean(o * o, axis=0, keepdims=True)
    o = o * lax.rsqrt(ms + EPS) * g_ref[...] * (1.0 - lam_init)
    return (o.T * sg.astype(F32)).astype(BF16)


def _diff_lagged_kernel(t, tk, lam_init, q_ref, k0_ref, k1_ref, v_ref, bias_ref, lam_ref, g_ref, sg_ref,
                        o_ref, jump_ref, m_ref, acc_ref, done_ref, s_ref):
    L = k0_ref.shape[2]

    def score_fn(i, j):
        return _diff_scores(t, tk, i, j, q_ref[0, 0, :, _blk(i, t)], k0_ref, k1_ref, bias_ref)

    def v_chunk_fn(j):
        return v_ref[0, 0, :, _blk(j, tk)]

    def finish_fn(i, acc):
        rows = _blk(i, t)
        o_ref[0, rows, :] = _diff_finish(acc, t, lam_init, lam_ref, g_ref, sg_ref[0, rows, :])

    _flash_lagged_tiles(L // t, L // tk, score_fn, v_chunk_fn, finish_fn, m_ref, acc_ref, done_ref,
                        jump_ref.at[0], s_ref)


def _diff_exact_kernel(t, tk, lam_init, q_ref, k0_ref, k1_ref, v_ref, bias_ref, lam_ref, g_ref, sg_ref,
                       o_ref, m_ref, acc_ref, s0_ref, s1_ref, mp0_ref, mp1_ref):
    i = pl.program_id(2)
    L = k0_ref.shape[2]
    qt = q_ref[0, 0]

    def score_fn(j):
        return _diff_scores(t, tk, i, j, qt, k0_ref, k1_ref, bias_ref)

    def v_chunk_fn(j):
        return v_ref[0, 0, :, _blk(j, tk)]

    _flash_pipeline(L // tk, score_fn, v_chunk_fn, m_ref, acc_ref, (s0_ref, s1_ref), (mp0_ref, mp1_ref))
    o_ref[0] = _diff_finish(acc_ref[...], t, lam_init, lam_ref, g_ref, sg_ref[0])


def _diff_attn(qT, k, vT, bias_tiles, lam_params, subln_g, sg, lam_init, t, tk):
    B, _, _, L = qT.shape
    hw = 2 * HEAD_DIM
    nt = L // t
    g_tab = jnp.broadcast_to(subln_g[:, None], (hw, t))
    y_shape = jax.ShapeDtypeStruct((B, L, DIFF_HEADS * hw), BF16)

    def run(lagged):
        nd = 2 if lagged else 3

        def spec(block, index):
            return pl.BlockSpec(block, (lambda b, h: index(b, h, 0)) if lagged else index)

        tq_blk = L if lagged else t
        y_spec = spec((1, tq_blk, hw), lambda b, h, i: (b, i, h))
        in_specs = [spec((1, 1, hw, tq_blk), lambda b, h, i: (b, h, 0, i)),
                    spec((1, 1, L, HEAD_DIM), lambda b, h, i: (b, 2 * h, 0, 0)),
                    spec((1, 1, L, HEAD_DIM), lambda b, h, i: (b, 2 * h + 1, 0, 0)),
                    spec((1, 1, hw + ONES_ROWS, L), lambda b, h, i: (b, h, 0, 0)),
                    spec((1, 5, t, t), lambda b, h, i: (h, 0, 0, 0)),
                    spec((4, HEAD_DIM), lambda b, h, i: (0, 0)),
                    spec((hw, t), lambda b, h, i: (0, 0)),
                    y_spec]
        args = (qT, k, k, vT, bias_tiles, lam_params, g_tab, sg)
        if lagged:
            return pl.pallas_call(
                functools.partial(_diff_lagged_kernel, t, tk, lam_init),
                grid=(B, DIFF_HEADS),
                in_specs=in_specs,
                out_specs=(y_spec, pl.BlockSpec((1, 1, 2 * t), lambda b, h: (b * DIFF_HEADS + h, 0, 0))),
                out_shape=(y_shape, jax.ShapeDtypeStruct((B * DIFF_HEADS, 1, 2 * t), F32)),
                scratch_shapes=_lagged_scratch(hw, tk, 2 * t),
                compiler_params=_cparams(("parallel",) * nd),
                name="diff_attn",
            )(*args)
        return (pl.pallas_call(
            functools.partial(_diff_exact_kernel, t, tk, lam_init),
            grid=(B, DIFF_HEADS, nt),
            in_specs=in_specs,
            out_specs=y_spec,
            out_shape=y_shape,
            scratch_shapes=_exact_scratch(hw, tk, 2 * t),
            compiler_params=_cparams(("parallel",) * nd),
            name="diff_attn_exact",
        )(*args),)

    return _with_exact_fallback(run)[0]


def _merge_kernel(final, x_ref, ng_ref, yh_ref, yg_ref, yd_ref, wm_ref, bm_ref, wh_ref, wg_ref, wd_ref,
                  wo_ref, fg_ref, o_ref):
    x = x_ref[...]
    ms = jnp.mean(x * x, axis=-1, keepdims=True)
    h = (x * lax.rsqrt(ms + EPS) * ng_ref[...]).astype(BF16)
    merged = None
    for b, (y_ref, w_ref) in enumerate(((yh_ref, wh_ref), (yg_ref, wg_ref), (yd_ref, wd_ref))):
        z = _dotf(h, wm_ref[:, b * D_MODEL:(b + 1) * D_MODEL]) + bm_ref[:, b * D_MODEL:(b + 1) * D_MODEL]
        gate = 1.0 / (1.0 + jnp.exp(-z))
        term = gate * _dotf(y_ref[...], w_ref[...])
        merged = term if merged is None else merged + term
    y = x + _dotf(merged.astype(BF16), wo_ref[...])
    if final:
        ms = jnp.mean(y * y, axis=-1, keepdims=True)
        y = y * lax.rsqrt(ms + EPS) * fg_ref[...]
    o_ref[...] = y


def _merge_out(x2, norm_g, y_hy, y_gq, y_df, w_merge, b_merge, w_hy, w_gq, w_df, w_out, final_g, final, tm):
    T = x2.shape[0]
    full = lambda a: pl.BlockSpec(a.shape, lambda i: (0,) * a.ndim)
    tok = lambda w: pl.BlockSpec((tm, w), lambda i: (i, 0))
    ng = norm_g.reshape(1, D_MODEL)
    bm = b_merge.reshape(1, -1)
    fg = final_g.reshape(1, D_MODEL)
    return pl.pallas_call(
        functools.partial(_merge_kernel, final),
        grid=(T // tm,),
        in_specs=[tok(D_MODEL), full(ng), tok(HY_WIDTH), tok(GQA_HEADS * HEAD_DIM), tok(DIFF_HEADS * 2 * HEAD_DIM),
                  full(w_merge), full(bm), full(w_hy), full(w_gq), full(w_df), full(w_out), full(fg)],
        out_specs=tok(D_MODEL),
        out_shape=jax.ShapeDtypeStruct((T, D_MODEL), F32),
        compiler_params=_cparams(("parallel",)),
        name="merge_out",
    )(x2, ng, y_hy, y_gq, y_df, w_merge, bm, w_hy, w_gq, w_df, w_out, fg)


def _rope_tables_t(L):
    rows = L // GRID_W
    row = jnp.broadcast_to(jnp.arange(rows, dtype=F32)[:, None], (rows, GRID_W)).reshape(L)
    col = jnp.broadcast_to(jnp.arange(GRID_W, dtype=F32)[None, :], (rows, GRID_W)).reshape(L)
    n_freq = HEAD_DIM // 4
    inv_freq = ROPE_THETA ** (-jnp.arange(n_freq, dtype=F32) / n_freq)
    ang = jnp.concatenate([row[:, None] * inv_freq, col[:, None] * inv_freq], axis=-1)
    return jnp.cos(ang).T, jnp.sin(ang).T


def _hyena_positions(L):
    t01 = jnp.linspace(0.0, 1.0, L, dtype=F32)[:, None]
    bands = (FILT_EMB - 1) // 2
    w = 2.0 * math.pi * jnp.arange(L, dtype=F32)[:, None] / L
    f = jnp.linspace(1e-4, bands - 1, bands, dtype=F32)[None, :]
    z = jnp.concatenate([t01, jnp.cos(f * w), -jnp.sin(f * w)], axis=-1)
    z = jnp.pad(z, ((0, 0), (0, FILT_EMB_PAD - FILT_EMB)))
    max_decay = math.log(HY_TARGET) / HY_FAST_DECAY
    min_decay = math.log(HY_TARGET) / HY_SLOW_DECAY
    deltas = jnp.linspace(min_decay, max_decay, HY_WIDTH, dtype=F32)
    window = jnp.exp(-t01 * jnp.abs(deltas)[None, :]) + HY_SHIFT
    return z, window


def _tile(L, want):
    t = min(L, want)
    assert L % t == 0
    return t


def _trunk(x, p, bias_tiles, attn_t):
    B, L, _ = x.shape
    depth = p['w_in'].shape[0]
    cos_t, sin_t = _rope_tables_t(L)
    z_pad, window = _hyena_positions(L)
    plan = _FftPlan(L)
    tm = _tile(L, 512)
    gqa_tq = _tile(L, 128)
    x2 = x.reshape(B * L, D_MODEL)
    for l in range(depth):
        (hy_u, hy_sg, gq_qT, gq_k, gq_vT, gq_sg, df_qT, df_k, df_vT, df_sg) = _in_proj(
            x2.reshape(B, L, D_MODEL), p['norm_g'][l], p['w_in_bf'][l], p['q_norm_g'][l], p['k_norm_g'][l],
            cos_t, sin_t, tm, gqa_tq)

        hv, x0g = _hy_pre(hy_u, p['hy_conv_w'][l], p['hy_conv_b'][l], hy_sg, _tile(L, 512))
        w1_pad = jnp.pad(p['hy_f_w1'][l], ((0, FILT_EMB_PAD - FILT_EMB), (0, 0)))
        kf, kb = _hy_filter(z_pad, window, w1_pad, p['hy_f_b1'][l], p['hy_f_w2'][l], p['hy_f_b2'][l],
                            p['hy_f_wout'][l], p['hy_f_freq'][l], _tile(L, 512))
        spec = _hy_spectrum(plan, kf, kb)
        y_hy = _hy_conv(plan, hv, spec, x0g, p['hy_bias'][l])

        y_gq = _gqa_attn(gq_qT, gq_k, gq_vT, gq_sg, gqa_tq, _tile(L // 2, 512))

        lam_init = 0.8 - 0.6 * math.exp(-0.3 * l)
        lam_params = jnp.stack([p['lam_q1'][l], p['lam_k1'][l], p['lam_q2'][l], p['lam_k2'][l]])
        y_df = _diff_attn(df_qT, df_k, df_vT, bias_tiles, lam_params, p['diff_subln_g'][l], df_sg, lam_init,
                          attn_t, max(attn_t, _tile(L // 2, 512)))

        x2 = _merge_out(x2, p['norm_g'][l], y_hy.reshape(B * L, -1), y_gq.reshape(B * L, -1),
                        y_df.reshape(B * L, -1), p['w_merge_bf'][l], p['b_merge'][l], p['w_branch_hy_bf'][l],
                        p['w_branch_gqa_bf'][l], p['w_branch_diff_bf'][l], p['w_out_bf'][l], p['final_g'],
                        l == depth - 1, _tile(B * L, 512))
    return x2.reshape(B, L, D_MODEL)


def kernel(x_prompt, x_sample, rel_bias, norm_g, w_in, hy_conv_w, hy_conv_b, hy_f_w1, hy_f_b1, hy_f_w2, hy_f_b2, hy_f_wout, hy_f_freq, hy_bias, q_norm_g, k_norm_g, lam_q1, lam_k1, lam_q2, lam_k2, diff_subln_g, w_branch_hy, w_branch_gqa, w_branch_diff, w_merge, b_merge, w_out, final_g):
    p = dict(norm_g=norm_g, hy_conv_w=hy_conv_w, hy_conv_b=hy_conv_b, hy_f_w1=hy_f_w1,
             hy_f_b1=hy_f_b1, hy_f_w2=hy_f_w2, hy_f_b2=hy_f_b2, hy_f_wout=hy_f_wout, hy_f_freq=hy_f_freq,
             hy_bias=hy_bias, q_norm_g=q_norm_g, k_norm_g=k_norm_g, lam_q1=lam_q1, lam_k1=lam_k1,
             lam_q2=lam_q2, lam_k2=lam_k2, diff_subln_g=diff_subln_g, b_merge=b_merge, final_g=final_g,
             w_in=w_in,
             w_in_bf=w_in.astype(BF16), w_merge_bf=w_merge.astype(BF16), w_out_bf=w_out.astype(BF16),
             w_branch_hy_bf=w_branch_hy.astype(BF16), w_branch_gqa_bf=w_branch_gqa.astype(BF16),
             w_branch_diff_bf=w_branch_diff.astype(BF16))
    outs = []
    for x in (x_prompt, x_sample):
        attn_t = _tile(x.shape[1], 256)
        bias_tiles = _t5_bias_tiles(rel_bias, attn_t)
        outs.append(_trunk(x, p, bias_tiles, attn_t))
    return tuple(outs)
```

```python
import functools
import math

import numpy as np
import jax
import jax.numpy as jnp
from jax import lax
from jax.experimental import pallas as pl
from jax.experimental.pallas import tpu as pltpu

D_MODEL = 1024
HEAD_DIM = 64
EPS = 1e-6
GRID_W = 64
ROPE_THETA = 10000.0

HY_WIDTH = 512
FILT_EMB = 33
FILT_EMB_PAD = 64
FILT_ORDER = 64
FILT_INNER = 2
HY_FAST_DECAY = 0.3
HY_SLOW_DECAY = 1.5
HY_TARGET = 1e-2
HY_SHIFT = 0.0

GQA_HEADS = 8
GQA_KV_HEADS = 2
GQA_GROUP = GQA_HEADS // GQA_KV_HEADS
DIFF_HEADS = 4
REL_BUCKETS = 32
REL_MAX_DIST = 128

C_HY_U = 0
C_HY_G = 1536
C_GQ_Q = 2048
C_GQ_K = 2560
C_GQ_V = 2688
C_GQ_G = 2816
C_DF_Q = 3328
C_DF_K = 3840
C_DF_V = 4352
C_DF_G = 4864
IN_COLS = 5376

LANES = 128
ONES_ROWS = 16
VMEM_LIMIT = 56 * 1024 * 1024
LOG2E = math.log2(math.e)
HI = lax.Precision.HIGHEST
F32 = jnp.float32
BF16 = jnp.bfloat16


def _cparams(sem):
    return pltpu.CompilerParams(dimension_semantics=sem, vmem_limit_bytes=VMEM_LIMIT)


def _silu(x):
    return x * (1.0 / (1.0 + jnp.exp(-x)))


def _dotf(a, b):
    return jnp.dot(a, b, preferred_element_type=F32)


def _norm_rope_t(xt, g_tab, cos, sin):
    ms = jnp.mean(xt * xt, axis=1, keepdims=True)
    xn = xt * lax.rsqrt(ms + EPS) * g_tab[None]
    half = HEAD_DIM // 2
    x1, x2 = xn[:, :half, :], xn[:, half:, :]
    c, s = cos[None], sin[None]
    return jnp.concatenate([x1 * c - x2 * s, x2 * c + x1 * s], axis=1)


def _in_proj_kernel(tq, x_ref, ng_ref, w_ref, qg_ref, kg_ref, cos_ref, sin_ref,
                    hyu_ref, hysg_ref, gqq_ref, gqk_ref, gqv_ref, gqsg_ref,
                    dfq_ref, dfk_ref, dfv_ref, dfsg_ref):
    x = x_ref[0]
    tm = x.shape[0]
    ms = jnp.mean(x * x, axis=-1, keepdims=True)
    h = (x * lax.rsqrt(ms + EPS) * ng_ref[...]).astype(BF16)

    def proj(lo, hi):
        return _dotf(h, w_ref[:, lo:hi])

    hyu_ref[0] = proj(C_HY_U, C_HY_G)
    hysg_ref[0] = _silu(proj(C_HY_G, C_GQ_Q)).astype(BF16)

    cos, sin = cos_ref[...], sin_ref[...]
    scale = HEAD_DIM ** -0.5
    qt = proj(C_GQ_Q, C_GQ_K).T.reshape(GQA_HEADS, HEAD_DIM, tm)
    qn = (_norm_rope_t(qt, qg_ref[...], cos, sin) * (scale * LOG2E)).astype(BF16)
    for g in range(GQA_KV_HEADS):
        for a in range(tm // tq):
            for hh in range(GQA_GROUP):
                c0 = (a * GQA_GROUP + hh) * tq
                gqq_ref[0, g, :, c0:c0 + tq] = qn[g * GQA_GROUP + hh][:, a * tq:(a + 1) * tq]
    kt = proj(C_GQ_K, C_GQ_V).T.reshape(GQA_KV_HEADS, HEAD_DIM, tm)
    kt = _norm_rope_t(kt, kg_ref[...], cos, sin)
    for g in range(GQA_KV_HEADS):
        gqk_ref[0, g] = kt[g].T.astype(BF16)
    ones_rows = (lax.broadcasted_iota(jnp.int32, (ONES_ROWS, tm), 0) == 0).astype(BF16)
    vt = proj(C_GQ_V, C_GQ_G).T.astype(BF16)
    for g in range(GQA_KV_HEADS):
        gqv_ref[0, g, :HEAD_DIM, :] = vt[g * HEAD_DIM:(g + 1) * HEAD_DIM]
        gqv_ref[0, g, HEAD_DIM:, :] = ones_rows
    gqsg_ref[0] = _silu(proj(C_GQ_G, C_DF_Q)).astype(BF16)

    dfq_ref[0] = (proj(C_DF_Q, C_DF_K) * (scale * LOG2E)).T.reshape(DIFF_HEADS, 2 * HEAD_DIM, tm).astype(BF16)
    dk = proj(C_DF_K, C_DF_V).astype(BF16)
    for j in range(2 * DIFF_HEADS):
        dfk_ref[0, j] = dk[:, j * HEAD_DIM:(j + 1) * HEAD_DIM]
    dvt = proj(C_DF_V, C_DF_G).T.astype(BF16)
    for hh in range(DIFF_HEADS):
        dfv_ref[0, hh, :2 * HEAD_DIM, :] = dvt[hh * 2 * HEAD_DIM:(hh + 1) * 2 * HEAD_DIM]
        dfv_ref[0, hh, 2 * HEAD_DIM:, :] = ones_rows
    dfsg_ref[0] = _silu(proj(C_DF_G, IN_COLS)).astype(BF16)


def _in_proj(x, norm_g, w_in_bf, q_g, k_g, cos_t, sin_t, tm, tq):
    B, L, _ = x.shape
    assert tm % tq == 0
    nt = L // tm
    qg_tab = jnp.broadcast_to(q_g[:, None], (HEAD_DIM, tm))
    kg_tab = jnp.broadcast_to(k_g[:, None], (HEAD_DIM, tm))
    full = lambda shape: pl.BlockSpec(shape, lambda b, i: (0,) * len(shape))
    out_shapes = (
        jax.ShapeDtypeStruct((B, L, 3 * HY_WIDTH), F32),
        jax.ShapeDtypeStruct((B, L, HY_WIDTH), BF16),
        jax.ShapeDtypeStruct((B, GQA_KV_HEADS, HEAD_DIM, GQA_GROUP * L), BF16),
        jax.ShapeDtypeStruct((B, GQA_KV_HEADS, L, HEAD_DIM), BF16),
        jax.ShapeDtypeStruct((B, GQA_KV_HEADS, HEAD_DIM + ONES_ROWS, L), BF16),
        jax.ShapeDtypeStruct((B, L, GQA_HEADS * HEAD_DIM), BF16),
        jax.ShapeDtypeStruct((B, DIFF_HEADS, 2 * HEAD_DIM, L), BF16),
        jax.ShapeDtypeStruct((B, 2 * DIFF_HEADS, L, HEAD_DIM), BF16),
        jax.ShapeDtypeStruct((B, DIFF_HEADS, 2 * HEAD_DIM + ONES_ROWS, L), BF16),
        jax.ShapeDtypeStruct((B, L, DIFF_HEADS * 2 * HEAD_DIM), BF16),
    )
    tok = lambda w: pl.BlockSpec((1, tm, w), lambda b, i: (b, i, 0))
    tr = lambda h, d: pl.BlockSpec((1, h, d, tm), lambda b, i: (b, 0, 0, i))
    rows = lambda h: pl.BlockSpec((1, h, tm, HEAD_DIM), lambda b, i: (b, 0, i, 0))
    return pl.pallas_call(
        functools.partial(_in_proj_kernel, tq),
        grid=(B, nt),
        in_specs=[
            tok(D_MODEL),
            full((1, D_MODEL)),
            full((D_MODEL, IN_COLS)),
            full((HEAD_DIM, tm)),
            full((HEAD_DIM, tm)),
            pl.BlockSpec((HEAD_DIM // 2, tm), lambda b, i: (0, i)),
            pl.BlockSpec((HEAD_DIM // 2, tm), lambda b, i: (0, i)),
        ],
        out_specs=(
            tok(3 * HY_WIDTH), tok(HY_WIDTH),
            pl.BlockSpec((1, GQA_KV_HEADS, HEAD_DIM, GQA_GROUP * tm), lambda b, i: (b, 0, 0, i)),
            rows(GQA_KV_HEADS), tr(GQA_KV_HEADS, HEAD_DIM + ONES_ROWS),
            tok(GQA_HEADS * HEAD_DIM),
            tr(DIFF_HEADS, 2 * HEAD_DIM), rows(2 * DIFF_HEADS), tr(DIFF_HEADS, 2 * HEAD_DIM + ONES_ROWS),
            tok(DIFF_HEADS * 2 * HEAD_DIM),
        ),
        out_shape=out_shapes,
        compiler_params=_cparams(("parallel", "parallel")),
        name="in_proj",
    )(x, norm_g.reshape(1, D_MODEL), w_in_bf, qg_tab, kg_tab, cos_t, sin_t)


def _hy_pre_kernel(u_ref, prev_ref, next_ref, w_ref, b_ref, sg_ref, hv_ref, x0g_ref):
    i = pl.program_id(1)
    nt = pl.num_programs(1)
    u = u_ref[0]
    tl = u.shape[0]
    prev_row = jnp.where(i > 0, prev_ref[0, 0, 7:8, :], 0.0)
    next_row = jnp.where(i < nt - 1, next_ref[0, 0, 0:1, :], 0.0)
    row = lax.broadcasted_iota(jnp.int32, u.shape, 0)
    up = jnp.where(row == 0, prev_row, pltpu.roll(u, 1, 0))
    dn = jnp.where(row == tl - 1, next_row, pltpu.roll(u, tl - 1, 0))
    w = w_ref[...]
    hy = up * w[0:1] + u * w[1:2] + dn * w[2:3] + b_ref[...]
    x0 = hy[:, :HY_WIDTH]
    x1 = hy[:, HY_WIDTH:2 * HY_WIDTH]
    hv = hy[:, 2 * HY_WIDTH:]
    hv_ref[0] = hv * x1
    x0g_ref[0] = (x0 * sg_ref[0].astype(F32)).astype(x0g_ref.dtype)


def _hy_pre(hy_u, conv_w, conv_b, hy_sg, tl):
    B, L, W3 = hy_u.shape
    nt = L // tl
    g8 = tl // 8
    u4 = hy_u.reshape(B, L // 8, 8, W3)
    return pl.pallas_call(
        _hy_pre_kernel,
        grid=(B, nt),
        in_specs=[
            pl.BlockSpec((1, tl, W3), lambda b, i: (b, i, 0)),
            pl.BlockSpec((1, 1, 8, W3), lambda b, i: (b, jnp.maximum(i * g8 - 1, 0), 0, 0)),
            pl.BlockSpec((1, 1, 8, W3), lambda b, i: (b, jnp.minimum((i + 1) * g8, L // 8 - 1), 0, 0)),
            pl.BlockSpec((3, W3), lambda b, i: (0, 0)),
            pl.BlockSpec((1, W3), lambda b, i: (0, 0)),
            pl.BlockSpec((1, tl, HY_WIDTH), lambda b, i: (b, i, 0)),
        ],
        out_specs=(
            pl.BlockSpec((1, tl, HY_WIDTH), lambda b, i: (b, i, 0)),
            pl.BlockSpec((1, tl, HY_WIDTH), lambda b, i: (b, i, 0)),
        ),
        out_shape=(
            jax.ShapeDtypeStruct((B, L, HY_WIDTH), F32),
            jax.ShapeDtypeStruct((B, L, HY_WIDTH), BF16),
        ),
        compiler_params=_cparams(("parallel", "parallel")),
        name="hy_pre",
    )(hy_u, u4, u4, conv_w, conv_b.reshape(1, W3), hy_sg)


def _hy_filter_kernel(z_ref, win_ref, w1_ref, b1_ref, w2_ref, b2_ref, wout_ref, freq_ref, kf_ref, kb_ref):
    i = pl.program_id(0)
    freq = freq_ref[...]
    a = jnp.sin(freq * (jnp.dot(z_ref[...], w1_ref[...], precision=HI, preferred_element_type=F32) + b1_ref[...]))
    for j in range(FILT_INNER):
        a = jnp.sin(freq * (jnp.dot(a, w2_ref[j], precision=HI, preferred_element_type=F32) + b2_ref[j:j + 1, :]))
    hf = jnp.dot(a, wout_ref[...], precision=HI, preferred_element_type=F32)
    win = win_ref[...]
    kf_ref[...] = hf[:, :HY_WIDTH] * win
    row = lax.broadcasted_iota(jnp.int32, win.shape, 0)
    kb_ref[...] = jnp.where((row == 0) & (i == 0), 0.0, hf[:, HY_WIDTH:] * win)


def _hy_filter(z_pad, window, w1_pad, b1, w2, b2, wout, freq, tl):
    L = z_pad.shape[0]
    full = lambda shape: pl.BlockSpec(shape, lambda i: (0,) * len(shape))
    return pl.pallas_call(
        _hy_filter_kernel,
        grid=(L // tl,),
        in_specs=[
            pl.BlockSpec((tl, FILT_EMB_PAD), lambda i: (i, 0)),
            pl.BlockSpec((tl, HY_WIDTH), lambda i: (i, 0)),
            full((FILT_EMB_PAD, FILT_ORDER)),
            full((1, FILT_ORDER)),
            full((FILT_INNER, FILT_ORDER, FILT_ORDER)),
            full((FILT_INNER, FILT_ORDER)),
            full((FILT_ORDER, 2 * HY_WIDTH)),
            full((1, FILT_ORDER)),
        ],
        out_specs=(
            pl.BlockSpec((tl, HY_WIDTH), lambda i: (i, 0)),
            pl.BlockSpec((tl, HY_WIDTH), lambda i: (i, 0)),
        ),
        out_shape=(
            jax.ShapeDtypeStruct((L, HY_WIDTH), F32),
            jax.ShapeDtypeStruct((L, HY_WIDTH), F32),
        ),
        compiler_params=_cparams(("parallel",)),
        name="hy_filter",
    )(z_pad, window, w1_pad, b1.reshape(1, -1), w2, b2, wout, freq.reshape(1, -1))


class _FftPlan:
    def __init__(self, L):
        n = 2 * L
        e = int(round(math.log2(n)))
        assert 2 ** e == n and e >= 8
        self.L, self.n = L, n
        self.n1 = 2 ** (e // 2)
        self.n2 = n // self.n1
        self.h1 = self.n1 // 2
        self.kh = self.h1 + 8
        self.kv = self.h1 + 1
        n1, n2, h1, kh = self.n1, self.n2, self.h1, self.kh
        k1 = np.arange(kh)[:, None]
        a = 2 * np.pi * k1 * np.arange(h1)[None, :] / n1
        self.fa = np.concatenate([np.cos(a), -np.sin(a)], axis=0).astype(np.float32)
        a = 2 * np.pi * np.arange(n2)[:, None] * np.arange(n2)[None, :] / n2
        self.f2r, self.f2i = np.cos(a).astype(np.float32), (-np.sin(a)).astype(np.float32)
        a = 2 * np.pi * k1 * np.arange(n2)[None, :] / n
        self.twr, self.twi = np.cos(a).astype(np.float32), (-np.sin(a)).astype(np.float32)

    def fwd_tables(self):
        return [jnp.asarray(t) for t in (self.fa, self.f2r, self.f2i, self.twr, self.twi)]

    def conv_tables(self):
        n, n1, n2, h1, kh = self.n, self.n1, self.n2, self.h1, self.kh
        k1 = np.arange(kh)[None, :, None]
        a = 2 * np.pi * k1 * (np.arange(h1)[None, None, :] / n1 + np.arange(n2)[:, None, None] / n)
        fa_tw = np.concatenate([np.cos(a), -np.sin(a)], axis=1)
        f2r, f2i = self.f2r.astype(np.float64), self.f2i.astype(np.float64)
        fwd = np.block([[f2r, -f2i], [f2i, f2r]])
        inv = np.block([[f2r, f2i], [-f2i, f2r]])
        wgt = np.where(np.arange(kh) <= h1, 2.0, 0.0)
        wgt[0] = 1.0
        wgt[h1] = 1.0
        k1 = np.arange(kh)[None, None, :]
        a = 2 * np.pi * k1 * (np.arange(h1)[None, :, None] / n1 + np.arange(n2)[:, None, None] / n)
        g = np.concatenate([np.cos(a) * wgt / n, -np.sin(a) * wgt / n], axis=2)
        return [jnp.asarray(t, dtype=BF16) for t in (fa_tw, fwd, inv, g)]


def _dot_exact(a, b):
    ah, bh = a.astype(BF16), b.astype(BF16)
    al, bl = (a - ah.astype(F32)).astype(BF16), (b - bh.astype(F32)).astype(BF16)
    return _dotf(ah, bh) + (_dotf(ah, bl) + _dotf(al, bh))


def _dft_cols(plan, x_ref, fa_ref, pr_ref, pi_ref, dot):
    fa = fa_ref[...]

    def body(n2, _):
        x = x_ref[pl.ds(n2, plan.h1, stride=plan.n2), :]
        y = dot(fa, x)
        pr_ref[pl.ds(n2, plan.kh, stride=plan.n2), :] = y[:plan.kh]
        pi_ref[pl.ds(n2, plan.kh, stride=plan.n2), :] = y[plan.kh:]
        return 0

    lax.fori_loop(0, plan.n2, body, 0, unroll=4)


def _twiddled_row_dft(plan, k1, f2r, f2i, twr_ref, twi_ref, ar, ai, dot):
    twr = twr_ref[pl.ds(k1, 1), :]
    twi = twi_ref[pl.ds(k1, 1), :]
    mr = f2r * twr - f2i * twi
    mi = f2r * twi + f2i * twr
    m = jnp.concatenate([jnp.concatenate([mr, -mi], axis=1), jnp.concatenate([mi, mr], axis=1)], axis=0)
    y = dot(m, jnp.concatenate([ar, ai], axis=0))
    return y[:plan.n2], y[plan.n2:]


def _hy_spectrum_kernel(plan, kf_ref, kb_ref, fa_ref, f2r_ref, f2i_ref, twr_ref, twi_ref, c_ref, pr_ref, pi_ref):
    f2r, f2i = f2r_ref[...], f2i_ref[...]
    n2 = plan.n2
    for which, src in enumerate((kf_ref, kb_ref)):
        _dft_cols(plan, src, fa_ref, pr_ref, pi_ref, _dot_exact)

        def body(k1, _):
            r0 = pl.multiple_of(k1 * n2, n2)
            xr, xi = _twiddled_row_dft(plan, k1, f2r, f2i, twr_ref, twi_ref,
                                       pr_ref[pl.ds(r0, n2), :], pi_ref[pl.ds(r0, n2), :], _dot_exact)
            if which == 0:
                c_ref[0, pl.ds(r0, n2), :] = xr
                c_ref[1, pl.ds(r0, n2), :] = xi
            else:
                c_ref[0, pl.ds(r0, n2), :] = c_ref[0, pl.ds(r0, n2), :] + xr
                c_ref[1, pl.ds(r0, n2), :] = c_ref[1, pl.ds(r0, n2), :] - xi
            return 0

        lax.fori_loop(0, plan.kh, body, 0, unroll=2)


def _hy_spectrum(plan, kf, kb):
    rows = plan.kh * plan.n2
    nc = HY_WIDTH // LANES
    tabs = plan.fwd_tables()
    full = lambda a: pl.BlockSpec(a.shape, lambda c: (0,) * a.ndim)
    return pl.pallas_call(
        functools.partial(_hy_spectrum_kernel, plan),
        grid=(nc,),
        in_specs=[pl.BlockSpec((plan.L, LANES), lambda c: (0, c)),
                  pl.BlockSpec((plan.L, LANES), lambda c: (0, c))] + [full(t) for t in tabs],
        out_specs=pl.BlockSpec((2, rows, LANES), lambda c: (0, 0, c)),
        out_shape=jax.ShapeDtypeStruct((2, rows, HY_WIDTH), F32),
        scratch_shapes=[pltpu.VMEM((rows, LANES), F32), pltpu.VMEM((rows, LANES), F32)],
        compiler_params=_cparams(("parallel",)),
        name="hy_spectrum",
    )(kf, kb, *tabs)


SUBLANES = 8


def _hy_conv_kernel(plan, hv_ref, c_ref, x0g_ref, bias_ref, fa_ref, fwd_ref, inv_ref, g_ref, o_ref,
                    pr_ref, pi_ref, qr_ref, qi_ref, y_ref):
    n2, kh, h1 = plan.n2, plan.kh, plan.h1
    eight = lambda a: pl.ds(pl.multiple_of(a * SUBLANES, SUBLANES), SUBLANES)

    def cols_body(a, _):
        slab = jnp.swapaxes(hv_ref[0, :, eight(a), :], 0, 1).astype(BF16)
        for s in range(SUBLANES):
            j = a * SUBLANES + s
            y = _dotf(fa_ref[j], slab[s])
            pr_ref[j] = y[:kh]
            pi_ref[j] = y[kh:]
        return 0

    lax.fori_loop(0, n2 // SUBLANES, cols_body, 0, unroll=2)

    fwd, inv = fwd_ref[...], inv_ref[...]

    def rows_body(c, _):
        xr8 = jnp.swapaxes(pr_ref[:, eight(c), :], 0, 1)
        xi8 = jnp.swapaxes(pi_ref[:, eight(c), :], 0, 1)
        for d in range(0, SUBLANES, 2):
            k1 = c * SUBLANES + d
            x = jnp.concatenate([jnp.concatenate([xr8[d], xr8[d + 1]], axis=1),
                                 jnp.concatenate([xi8[d], xi8[d + 1]], axis=1)], axis=0)
            xf = _dotf(fwd, x.astype(BF16))
            xr, xi = xf[:n2], xf[n2:]
            cr = jnp.concatenate([c_ref[0, k1], c_ref[0, k1 + 1]], axis=1)
            ci = jnp.concatenate([c_ref[1, k1], c_ref[1, k1 + 1]], axis=1)
            z = jnp.concatenate([xr * cr - xi * ci, xr * ci + xi * cr], axis=0)
            b = _dotf(inv, z.astype(BF16))
            w = b.shape[1] // 2
            for e in range(2):
                qr_ref[k1 + e] = b[:n2, e * w:(e + 1) * w]
                qi_ref[k1 + e] = b[n2:, e * w:(e + 1) * w]
        return 0

    trips = kh // SUBLANES
    lax.fori_loop(0, trips, rows_body, 0, unroll=next(u for u in (3, 5, 2, 1) if trips % u == 0))

    def icols_body(a, _):
        br8 = jnp.swapaxes(qr_ref[:, eight(a), :], 0, 1)
        bi8 = jnp.swapaxes(qi_ref[:, eight(a), :], 0, 1)
        ys = [_dotf(g_ref[a * SUBLANES + s], jnp.concatenate([br8[s], bi8[s]], axis=0).astype(BF16))
              for s in range(SUBLANES)]
        y_ref[:, eight(a), :] = jnp.swapaxes(jnp.stack(ys, axis=0), 0, 1)
        return 0

    lax.fori_loop(0, n2 // SUBLANES, icols_body, 0, unroll=2)

    bias = bias_ref[...]
    rows = max(1, 512 // n2)

    def out_body(t, _):
        r = pl.ds(t * rows, rows)
        y = y_ref[r] + hv_ref[0, r] * bias
        o_ref[0, r] = (y * x0g_ref[0, r].astype(F32)).astype(o_ref.dtype)
        return 0

    lax.fori_loop(0, h1 // rows, out_body, 0)


def _hy_conv(plan, hv, spec, x0g, hy_bias):
    B, L, _ = hv.shape
    n2, kh, h1 = plan.n2, plan.kh, plan.h1
    cb = LANES
    nc = HY_WIDTH // cb
    tabs = plan.conv_tables()
    full = lambda a: pl.BlockSpec(a.shape, lambda c, b: (0,) * a.ndim)
    view = lambda a: a.reshape(B, h1, n2, HY_WIDTH)
    seq = pl.BlockSpec((1, h1, n2, cb), lambda c, b: (b, 0, 0, c))
    seq_in = pl.BlockSpec((1, h1, n2, cb), lambda c, b: (b, 0, 0, c), pipeline_mode=pl.Buffered(1))
    out = pl.pallas_call(
        functools.partial(_hy_conv_kernel, plan),
        grid=(nc, B),
        in_specs=[seq_in,
                  pl.BlockSpec((2, kh, n2, cb), lambda c, b: (0, 0, 0, c), pipeline_mode=pl.Buffered(1)),
                  seq_in,
                  pl.BlockSpec((1, cb), lambda c, b: (0, c))] + [full(t) for t in tabs],
        out_specs=seq,
        out_shape=jax.ShapeDtypeStruct((B, h1, n2, HY_WIDTH), BF16),
        scratch_shapes=[pltpu.VMEM((n2, kh, cb), F32), pltpu.VMEM((n2, kh, cb), F32),
                        pltpu.VMEM((kh, n2, cb), F32), pltpu.VMEM((kh, n2, cb), F32),
                        pltpu.VMEM((h1, n2, cb), F32)],
        compiler_params=_cparams(("parallel", "parallel")),
        name="hy_conv",
    )(view(hv), spec.reshape(2, kh, n2, HY_WIDTH), view(x0g), hy_bias.reshape(1, HY_WIDTH), *tabs)
    return out.reshape(B, L, HY_WIDTH)


def _blk(index, size):
    if isinstance(index, int):
        return pl.ds(index * size, size)
    return pl.ds(pl.multiple_of(index * size, size), size)


def _chunks_per_trip(n):
    for u in (8, 4, 2):
        if n % u == 0:
            return u
    raise ValueError(f"need an even number of key chunks, got {n}")


LAG_JUMP_LIMIT = 32.0


def _flash_pipeline(n, score_fn, v_chunk_fn, m_ref, acc_ref, s_refs, mp_refs):
    u = _chunks_per_trip(n)
    m_ref[...] = jnp.full(m_ref.shape, -jnp.inf, F32)
    acc_ref[...] = jnp.zeros(acc_ref.shape, F32)

    def scores(j, slot):
        s = score_fn(j)
        s_refs[slot][...] = s
        mp_refs[slot][...] = jnp.max(s, axis=0, keepdims=True)

    def update(j, slot):
        m_old = m_ref[...]
        m_new = jnp.maximum(m_old, mp_refs[slot][...])
        alpha = jnp.exp2(m_old - m_new)
        p = jnp.exp2(s_refs[slot][...] - m_new).astype(BF16)
        acc_ref[...] = alpha * acc_ref[...] + _dotf(v_chunk_fn(j), p)
        m_ref[...] = m_new

    scores(0, 0)

    def body(jj, _):
        j = u * jj
        for d in range(u):
            scores(j + d + 1 if d + 1 < u else jnp.minimum(j + u, n - 1), (d + 1) % 2)
            update(j + d, d % 2)
        return 0

    lax.fori_loop(0, n // u, body, 0)


def _flash_lagged_tiles(nt, n, score_fn, v_chunk_fn, finish_fn, m_ref, acc_ref, done_ref, jump_ref, s_ref):
    s0 = score_fn(0, 0)
    s_ref[...] = s0
    m_ref[...] = jnp.max(s0, axis=0, keepdims=True)
    acc_ref[...] = jnp.zeros(acc_ref.shape, F32)
    done_ref[...] = jnp.ones(done_ref.shape, F32)
    jump_ref[...] = jnp.zeros(jump_ref.shape, F32)

    def step(j, s):
        m_used = m_ref[...]
        p = jnp.exp2(s - m_used).astype(BF16)
        rise = jnp.log2(jnp.max(p, axis=0, keepdims=True).astype(F32))
        up = jnp.maximum(rise, 0.0)
        acc_ref[...] = (acc_ref[...] + _dotf(v_chunk_fn(j), p)) * jnp.exp2(-up)
        jump_ref[...] = jnp.maximum(jump_ref[...], rise)
        m_ref[...] = m_used + up

    def body(i, _):
        s_cur = s_ref[...]
        for d in range(n):
            s_next = score_fn(i, d + 1) if d + 1 < n else score_fn(jnp.minimum(i + 1, nt - 1), 0)
            step(d, s_cur)
            if d == 0:
                finish_fn(jnp.maximum(i - 1, 0), done_ref[...])
            s_cur = s_next
        s_ref[...] = s_cur
        done_ref[...] = acc_ref[...]
        acc_ref[...] = jnp.zeros(acc_ref.shape, F32)
        m_ref[...] = jnp.max(s_cur, axis=0, keepdims=True)
        return 0

    lax.fori_loop(0, nt, body, 0)
    finish_fn(nt - 1, done_ref[...])


def _lagged_scratch(rows, tk, nq):
    acc = pltpu.VMEM((rows + ONES_ROWS, nq), F32)
    return [pltpu.VMEM((1, nq), F32), acc, acc, pltpu.VMEM((tk, nq), F32)]


def _exact_scratch(rows, tk, nq):
    return [pltpu.VMEM((1, nq), F32), pltpu.VMEM((rows + ONES_ROWS, nq), F32),
            pltpu.VMEM((tk, nq), F32), pltpu.VMEM((tk, nq), F32),
            pltpu.VMEM((1, nq), F32), pltpu.VMEM((1, nq), F32)]


def _with_exact_fallback(run):
    *outs, jump = run(True)
    outs = tuple(outs)
    return lax.cond(jnp.max(jump) > LAG_JUMP_LIMIT, lambda: tuple(run(False)), lambda: outs)


def _gqa_finish(acc, sg, tq):
    o = acc[:HEAD_DIM] * (1.0 / acc[HEAD_DIM:HEAD_DIM + 1])
    ot = jnp.concatenate([o[:, h * tq:(h + 1) * tq].T for h in range(GQA_GROUP)], axis=1)
    return (ot * sg.astype(F32)).astype(BF16)


def _gqa_lagged_kernel(tq, tk, q_ref, k_ref, v_ref, sg_ref, o_ref, jump_ref, m_ref, acc_ref, done_ref, s_ref):
    L = k_ref.shape[2]
    nq = GQA_GROUP * tq

    def score_fn(i, j):
        qt = q_ref[0, 0, :, _blk(i, nq)]
        return _dotf(k_ref[0, 0, _blk(j, tk), :], qt)

    def v_chunk_fn(j):
        return v_ref[0, 0, :, _blk(j, tk)]

    def finish_fn(i, acc):
        rows = _blk(i, tq)
        o_ref[0, rows, :] = _gqa_finish(acc, sg_ref[0, rows, :], tq)

    _flash_lagged_tiles(L // tq, L // tk, score_fn, v_chunk_fn, finish_fn, m_ref, acc_ref, done_ref,
                        jump_ref.at[0], s_ref)


def _gqa_exact_kernel(tk, q_ref, k_ref, v_ref, sg_ref, o_ref, m_ref, acc_ref, s0_ref, s1_ref, mp0_ref, mp1_ref):
    qt = q_ref[0, 0]
    tq = qt.shape[-1] // GQA_GROUP
    L = k_ref.shape[2]

    def score_fn(j):
        return _dotf(k_ref[0, 0, pl.ds(pl.multiple_of(j * tk, tk), tk), :], qt)

    def v_chunk_fn(j):
        return v_ref[0, 0, :, pl.ds(pl.multiple_of(j * tk, tk), tk)]

    _flash_pipeline(L // tk, score_fn, v_chunk_fn, m_ref, acc_ref, (s0_ref, s1_ref), (mp0_ref, mp1_ref))
    o_ref[0] = _gqa_finish(acc_ref[...], sg_ref[0], tq)


def _gqa_attn(qT, k, vT, sg, tq, tk):
    B, _, _, L4 = qT.shape
    L = L4 // GQA_GROUP
    gw = GQA_GROUP * HEAD_DIM
    nq = GQA_GROUP * tq
    nt = L // tq
    y_shape = jax.ShapeDtypeStruct((B, L, GQA_HEADS * HEAD_DIM), BF16)

    def run(lagged):
        if lagged:
            whole = lambda shape: pl.BlockSpec((1, 1) + shape, lambda b, g: (b, g, 0, 0))
            y_spec = pl.BlockSpec((1, L, gw), lambda b, g: (b, 0, g))
            return pl.pallas_call(
                functools.partial(_gqa_lagged_kernel, tq, tk),
                grid=(B, GQA_KV_HEADS),
                in_specs=[whole((HEAD_DIM, L4)), whole((L, HEAD_DIM)), whole((HEAD_DIM + ONES_ROWS, L)), y_spec],
                out_specs=(y_spec, pl.BlockSpec((1, 1, nq), lambda b, g: (b * GQA_KV_HEADS + g, 0, 0))),
                out_shape=(y_shape, jax.ShapeDtypeStruct((B * GQA_KV_HEADS, 1, nq), F32)),
                scratch_shapes=_lagged_scratch(HEAD_DIM, tk, nq),
                compiler_params=_cparams(("parallel", "parallel")),
                name="gqa_attn",
            )(qT, k, vT, sg)
        y_spec = pl.BlockSpec((1, tq, gw), lambda b, g, i: (b, i, g))
        return (pl.pallas_call(
            functools.partial(_gqa_exact_kernel, tk),
            grid=(B, GQA_KV_HEADS, nt),
            in_specs=[pl.BlockSpec((1, 1, HEAD_DIM, nq), lambda b, g, i: (b, g, 0, i)),
                      pl.BlockSpec((1, 1, L, HEAD_DIM), lambda b, g, i: (b, g, 0, 0)),
                      pl.BlockSpec((1, 1, HEAD_DIM + ONES_ROWS, L), lambda b, g, i: (b, g, 0, 0)),
                      y_spec],
            out_specs=y_spec,
            out_shape=y_shape,
            scratch_shapes=_exact_scratch(HEAD_DIM, tk, nq),
            compiler_params=_cparams(("parallel", "parallel", "parallel")),
            name="gqa_attn_exact",
        )(qT, k, vT, sg),)

    return _with_exact_fallback(run)[0]


def _t5_bias_kernel(t, rb_ref, o_ref):
    h = pl.program_id(0)
    kk = lax.broadcasted_iota(jnp.int32, (t, t), 0)
    qq = lax.broadcasted_iota(jnp.int32, (t, t), 1)
    nb = REL_BUCKETS // 2
    max_exact = nb // 2
    thresholds = [int(math.ceil(max_exact * (REL_MAX_DIST / max_exact) ** (j / (nb - max_exact)) - 1e-9))
                  for j in range(1, nb - max_exact)]
    for idx in range(5):
        rel = (idx - 2) * t + kk - qq
        n = jnp.abs(rel)
        large = jnp.full((t, t), max_exact, jnp.int32)
        for th in thresholds:
            large = large + (n >= th).astype(jnp.int32)
        bucket = jnp.where(rel > 0, nb, 0) + jnp.where(n < max_exact, n, large)
        val = jnp.zeros((t, t), F32)
        for b in range(REL_BUCKETS):
            val = jnp.where(bucket == b, rb_ref[h, b], val)
        o_ref[0, idx] = val * LOG2E


def _t5_bias_tiles(rel_bias, t):
    return pl.pallas_call(
        functools.partial(_t5_bias_kernel, t),
        grid=(DIFF_HEADS,),
        in_specs=[pl.BlockSpec(memory_space=pltpu.SMEM)],
        out_specs=pl.BlockSpec((1, 5, t, t), lambda h: (h, 0, 0, 0)),
        out_shape=jax.ShapeDtypeStruct((DIFF_HEADS, 5, t, t), F32),
        compiler_params=_cparams(("parallel",)),
        name="t5_bias_tiles",
    )(rel_bias.T)


def _diff_scores(t, tk, i, j, qt, k0_ref, k1_ref, bias_ref):
    tiles = tk // t
    bias = jnp.concatenate([bias_ref[0, jnp.clip(j * tiles + r - i, -2, 2) + 2] for r in range(tiles)],
                           axis=0)
    return jnp.concatenate(
        [_dotf(k0_ref[0, 0, _blk(j, tk), :], qt[:HEAD_DIM]) + bias,
         _dotf(k1_ref[0, 0, _blk(j, tk), :], qt[HEAD_DIM:]) + bias], axis=1)


def _diff_finish(acc, t, lam_init, lam_ref, g_ref, sg):
    hw = 2 * HEAD_DIM
    lp = lam_ref[...]
    lam = (jnp.exp(jnp.sum(lp[0:1] * lp[1:2], axis=1, keepdims=True))
           - jnp.exp(jnp.sum(lp[2:3] * lp[3:4], axis=1, keepdims=True)) + lam_init)
    o = acc[:hw] * (1.0 / acc[hw:hw + 1])
    o = o[:, :t] - lam * o[:, t:]
    ms = jnp.mean(o * o, axis=0, keepdims=True)
    o = o * lax.rsqrt(ms + EPS) * g_ref[...] * (1.0 - lam_init)
    return (o.T * sg.astype(F32)).astype(BF16)


def _diff_lagged_kernel(t, tk, lam_init, q_ref, k0_ref, k1_ref, v_ref, bias_ref, lam_ref, g_ref, sg_ref,
                        o_ref, jump_ref, m_ref, acc_ref, done_ref, s_ref):
    L = k0_ref.shape[2]

    def score_fn(i, j):
        return _diff_scores(t, tk, i, j, q_ref[0, 0, :, _blk(i, t)], k0_ref, k1_ref, bias_ref)

    def v_chunk_fn(j):
        return v_ref[0, 0, :, _blk(j, tk)]

    def finish_fn(i, acc):
        rows = _blk(i, t)
        o_ref[0, rows, :] = _diff_finish(acc, t, lam_init, lam_ref, g_ref, sg_ref[0, rows, :])

    _flash_lagged_tiles(L // t, L // tk, score_fn, v_chunk_fn, finish_fn, m_ref, acc_ref, done_ref,
                        jump_ref.at[0], s_ref)


def _diff_exact_kernel(t, tk, lam_init, q_ref, k0_ref, k1_ref, v_ref, bias_ref, lam_ref, g_ref, sg_ref,
                       o_ref, m_ref, acc_ref, s0_ref, s1_ref, mp0_ref, mp1_ref):
    i = pl.program_id(2)
    L = k0_ref.shape[2]
    qt = q_ref[0, 0]

    def score_fn(j):
        return _diff_scores(t, tk, i, j, qt, k0_ref, k1_ref, bias_ref)

    def v_chunk_fn(j):
        return v_ref[0, 0, :, _blk(j, tk)]

    _flash_pipeline(L // tk, score_fn, v_chunk_fn, m_ref, acc_ref, (s0_ref, s1_ref), (mp0_ref, mp1_ref))
    o_ref[0] = _diff_finish(acc_ref[...], t, lam_init, lam_ref, g_ref, sg_ref[0])


def _diff_attn(qT, k, vT, bias_tiles, lam_params, subln_g, sg, lam_init, t, tk):
    B, _, _, L = qT.shape
    hw = 2 * HEAD_DIM
    nt = L // t
    g_tab = jnp.broadcast_to(subln_g[:, None], (hw, t))
    y_shape = jax.ShapeDtypeStruct((B, L, DIFF_HEADS * hw), BF16)

    def run(lagged):
        nd = 2 if lagged else 3

        def spec(block, index):
            return pl.BlockSpec(block, (lambda b, h: index(b, h, 0)) if lagged else index)

        tq_blk = L if lagged else t
        y_spec = spec((1, tq_blk, hw), lambda b, h, i: (b, i, h))
        in_specs = [spec((1, 1, hw, tq_blk), lambda b, h, i: (b, h, 0, i)),
                    spec((1, 1, L, HEAD_DIM), lambda b, h, i: (b, 2 * h, 0, 0)),
                    spec((1, 1, L, HEAD_DIM), lambda b, h, i: (b, 2 * h + 1, 0, 0)),
                    spec((1, 1, hw + ONES_ROWS, L), lambda b, h, i: (b, h, 0, 0)),
                    spec((1, 5, t, t), lambda b, h, i: (h, 0, 0, 0)),
                    spec((4, HEAD_DIM), lambda b, h, i: (0, 0)),
                    spec((hw, t), lambda b, h, i: (0, 0)),
                    y_spec]
        args = (qT, k, k, vT, bias_tiles, lam_params, g_tab, sg)
        if lagged:
            return pl.pallas_call(
                functools.partial(_diff_lagged_kernel, t, tk, lam_init),
                grid=(B, DIFF_HEADS),
                in_specs=in_specs,
                out_specs=(y_spec, pl.BlockSpec((1, 1, 2 * t), lambda b, h: (b * DIFF_HEADS + h, 0, 0))),
                out_shape=(y_shape, jax.ShapeDtypeStruct((B * DIFF_HEADS, 1, 2 * t), F32)),
                scratch_shapes=_lagged_scratch(hw, tk, 2 * t),
                compiler_params=_cparams(("parallel",) * nd),
                name="diff_attn",
            )(*args)
        return (pl.pallas_call(
            functools.partial(_diff_exact_kernel, t, tk, lam_init),
            grid=(B, DIFF_HEADS, nt),
            in_specs=in_specs,
            out_specs=y_spec,
            out_shape=y_shape,
            scratch_shapes=_exact_scratch(hw, tk, 2 * t),
            compiler_params=_cparams(("parallel",) * nd),
            name="diff_attn_exact",
        )(*args),)

    return _with_exact_fallback(run)[0]


def _merge_kernel(final, x_ref, ng_ref, yh_ref, yg_ref, yd_ref, wm_ref, bm_ref, wh_ref, wg_ref, wd_ref,
                  wo_ref, fg_ref, o_ref):
    x = x_ref[...]
    ms = jnp.mean(x * x, axis=-1, keepdims=True)
    h = (x * lax.rsqrt(ms + EPS) * ng_ref[...]).astype(BF16)
    merged = None
    for b, (y_ref, w_ref) in enumerate(((yh_ref, wh_ref), (yg_ref, wg_ref), (yd_ref, wd_ref))):
        z = _dotf(h, wm_ref[:, b * D_MODEL:(b + 1) * D_MODEL]) + bm_ref[:, b * D_MODEL:(b + 1) * D_MODEL]
        gate = 1.0 / (1.0 + jnp.exp(-z))
        term = gate * _dotf(y_ref[...], w_ref[...])
        merged = term if merged is None else merged + term
    y = x + _dotf(merged.astype(BF16), wo_ref[...])
    if final:
        ms = jnp.mean(y * y, axis=-1, keepdims=True)
        y = y * lax.rsqrt(ms + EPS) * fg_ref[...]
    o_ref[...] = y


def _merge_out(x2, norm_g, y_hy, y_gq, y_df, w_merge, b_merge, w_hy, w_gq, w_df, w_out, final_g, final, tm):
    T = x2.shape[0]
    full = lambda a: pl.BlockSpec(a.shape, lambda i: (0,) * a.ndim)
    tok = lambda w: pl.BlockSpec((tm, w), lambda i: (i, 0))
    ng = norm_g.reshape(1, D_MODEL)
    bm = b_merge.reshape(1, -1)
    fg = final_g.reshape(1, D_MODEL)
    return pl.pallas_call(
        functools.partial(_merge_kernel, final),
        grid=(T // tm,),
        in_specs=[tok(D_MODEL), full(ng), tok(HY_WIDTH), tok(GQA_HEADS * HEAD_DIM), tok(DIFF_HEADS * 2 * HEAD_DIM),
                  full(w_merge), full(bm), full(w_hy), full(w_gq), full(w_df), full(w_out), full(fg)],
        out_specs=tok(D_MODEL),
        out_shape=jax.ShapeDtypeStruct((T, D_MODEL), F32),
        compiler_params=_cparams(("parallel",)),
        name="merge_out",
    )(x2, ng, y_hy, y_gq, y_df, w_merge, bm, w_hy, w_gq, w_df, w_out, fg)


def _rope_tables_t(L):
    rows = L // GRID_W
    row = jnp.broadcast_to(jnp.arange(rows, dtype=F32)[:, None], (rows, GRID_W)).reshape(L)
    col = jnp.broadcast_to(jnp.arange(GRID_W, dtype=F32)[None, :], (rows, GRID_W)).reshape(L)
    n_freq = HEAD_DIM // 4
    inv_freq = ROPE_THETA ** (-jnp.arange(n_freq, dtype=F32) / n_freq)
    ang = jnp.concatenate([row[:, None] * inv_freq, col[:, None] * inv_freq], axis=-1)
    return jnp.cos(ang).T, jnp.sin(ang).T


def _hyena_positions(L):
    t01 = jnp.linspace(0.0, 1.0, L, dtype=F32)[:, None]
    bands = (FILT_EMB - 1) // 2
    w = 2.0 * math.pi * jnp.arange(L, dtype=F32)[:, None] / L
    f = jnp.linspace(1e-4, bands - 1, bands, dtype=F32)[None, :]
    z = jnp.concatenate([t01, jnp.cos(f * w), -jnp.sin(f * w)], axis=-1)
    z = jnp.pad(z, ((0, 0), (0, FILT_EMB_PAD - FILT_EMB)))
    max_decay = math.log(HY_TARGET) / HY_FAST_DECAY
    min_decay = math.log(HY_TARGET) / HY_SLOW_DECAY
    deltas = jnp.linspace(min_decay, max_decay, HY_WIDTH, dtype=F32)
    window = jnp.exp(-t01 * jnp.abs(deltas)[None, :]) + HY_SHIFT
    return z, window


def _tile(L, want):
    t = min(L, want)
    assert L % t == 0
    return t


def _trunk(x, p, bias_tiles, attn_t):
    B, L, _ = x.shape
    depth = p['w_in'].shape[0]
    cos_t, sin_t = _rope_tables_t(L)
    z_pad, window = _hyena_positions(L)
    plan = _FftPlan(L)
    tm = _tile(L, 512)
    gqa_tq = _tile(L, 128)
    x2 = x.reshape(B * L, D_MODEL)
    for l in range(depth):
        (hy_u, hy_sg, gq_qT, gq_k, gq_vT, gq_sg, df_qT, df_k, df_vT, df_sg) = _in_proj(
            x2.reshape(B, L, D_MODEL), p['norm_g'][l], p['w_in_bf'][l], p['q_norm_g'][l], p['k_norm_g'][l],
            cos_t, sin_t, tm, gqa_tq)

        hv, x0g = _hy_pre(hy_u, p['hy_conv_w'][l], p['hy_conv_b'][l], hy_sg, _tile(L, 512))
        w1_pad = jnp.pad(p['hy_f_w1'][l], ((0, FILT_EMB_PAD - FILT_EMB), (0, 0)))
        kf, kb = _hy_filter(z_pad, window, w1_pad, p['hy_f_b1'][l], p['hy_f_w2'][l], p['hy_f_b2'][l],
                            p['hy_f_wout'][l], p['hy_f_freq'][l], _tile(L, 512))
        spec = _hy_spectrum(plan, kf, kb)
        y_hy = _hy_conv(plan, hv, spec, x0g, p['hy_bias'][l])

        y_gq = _gqa_attn(gq_qT, gq_k, gq_vT, gq_sg, gqa_tq, _tile(L // 2, 512))

        lam_init = 0.8 - 0.6 * math.exp(-0.3 * l)
        lam_params = jnp.stack([p['lam_q1'][l], p['lam_k1'][l], p['lam_q2'][l], p['lam_k2'][l]])
        y_df = _diff_attn(df_qT, df_k, df_vT, bias_tiles, lam_params, p['diff_subln_g'][l], df_sg, lam_init,
                          attn_t, max(attn_t, _tile(L // 2, 512)))

        x2 = _merge_out(x2, p['norm_g'][l], y_hy.reshape(B * L, -1), y_gq.reshape(B * L, -1),
                        y_df.reshape(B * L, -1), p['w_merge_bf'][l], p['b_merge'][l], p['w_branch_hy_bf'][l],
                        p['w_branch_gqa_bf'][l], p['w_branch_diff_bf'][l], p['w_out_bf'][l], p['final_g'],
                        l == depth - 1, _tile(B * L, 512))
    return x2.reshape(B, L, D_MODEL)


def kernel(x_prompt, x_sample, rel_bias, norm_g, w_in, hy_conv_w, hy_conv_b, hy_f_w1, hy_f_b1, hy_f_w2, hy_f_b2, hy_f_wout, hy_f_freq, hy_bias, q_norm_g, k_norm_g, lam_q1, lam_k1, lam_q2, lam_k2, diff_subln_g, w_branch_hy, w_branch_gqa, w_branch_diff, w_merge, b_merge, w_out, final_g):
    p = dict(norm_g=norm_g, hy_conv_w=hy_conv_w, hy_conv_b=hy_conv_b, hy_f_w1=hy_f_w1,
             hy_f_b1=hy_f_b1, hy_f_w2=hy_f_w2, hy_f_b2=hy_f_b2, hy_f_wout=hy_f_wout, hy_f_freq=hy_f_freq,
             hy_bias=hy_bias, q_norm_g=q_norm_g, k_norm_g=k_norm_g, lam_q1=lam_q1, lam_k1=lam_k1,
             lam_q2=lam_q2, lam_k2=lam_k2, diff_subln_g=diff_subln_g, b_merge=b_merge, final_g=final_g,
             w_in=w_in,
             w_in_bf=w_in.astype(BF16), w_merge_bf=w_merge.astype(BF16), w_out_bf=w_out.astype(BF16),
             w_branch_hy_bf=w_branch_hy.astype(BF16), w_branch_gqa_bf=w_branch_gqa.astype(BF16),
             w_branch_diff_bf=w_branch_diff.astype(BF16))
    outs = []
    for x in (x_prompt, x_sample):
        attn_t = _tile(x.shape[1], 256)
        bias_tiles = _t5_bias_tiles(rel_bias, attn_t)
        outs.append(_trunk(x, p, bias_tiles, attn_t))
    return tuple(outs)
```

```python
import functools
import math

import numpy as np
import jax
import jax.numpy as jnp
from jax import lax
from jax.experimental import pallas as pl
from jax.experimental.pallas import tpu as pltpu

D_MODEL = 1024
HEAD_DIM = 64
EPS = 1e-6
GRID_W = 64
ROPE_THETA = 10000.0

HY_WIDTH = 512
FILT_EMB = 33
FILT_EMB_PAD = 64
FILT_ORDER = 64
FILT_INNER = 2
HY_FAST_DECAY = 0.3
HY_SLOW_DECAY = 1.5
HY_TARGET = 1e-2
HY_SHIFT = 0.0

GQA_HEADS = 8
GQA_KV_HEADS = 2
GQA_GROUP = GQA_HEADS // GQA_KV_HEADS
DIFF_HEADS = 4
REL_BUCKETS = 32
REL_MAX_DIST = 128

C_HY_U = 0
C_HY_G = 1536
C_GQ_Q = 2048
C_GQ_K = 2560
C_GQ_V = 2688
C_GQ_G = 2816
C_DF_Q = 3328
C_DF_K = 3840
C_DF_V = 4352
C_DF_G = 4864
IN_COLS = 5376

LANES = 128
ONES_ROWS = 16
VMEM_LIMIT = 56 * 1024 * 1024
LOG2E = math.log2(math.e)
HI = lax.Precision.HIGHEST
F32 = jnp.float32
BF16 = jnp.bfloat16


def _cparams(sem):
    return pltpu.CompilerParams(dimension_semantics=sem, vmem_limit_bytes=VMEM_LIMIT)


def _silu(x):
    return x * (1.0 / (1.0 + jnp.exp(-x)))


def _dotf(a, b):
    return jnp.dot(a, b, preferred_element_type=F32)


def _norm_rope_t(xt, g_tab, cos, sin):
    ms = jnp.mean(xt * xt, axis=1, keepdims=True)
    xn = xt * lax.rsqrt(ms + EPS) * g_tab[None]
    half = HEAD_DIM // 2
    x1, x2 = xn[:, :half, :], xn[:, half:, :]
    c, s = cos[None], sin[None]
    return jnp.concatenate([x1 * c - x2 * s, x2 * c + x1 * s], axis=1)


def _in_proj_kernel(tq, x_ref, ng_ref, w_ref, qg_ref, kg_ref, cos_ref, sin_ref,
                    hyu_ref, hysg_ref, gqq_ref, gqk_ref, gqv_ref, gqsg_ref,
                    dfq_ref, dfk_ref, dfv_ref, dfsg_ref):
    x = x_ref[0]
    tm = x.shape[0]
    ms = jnp.mean(x * x, axis=-1, keepdims=True)
    h = (x * lax.rsqrt(ms + EPS) * ng_ref[...]).astype(BF16)

    def proj(lo, hi):
        return _dotf(h, w_ref[:, lo:hi])

    hyu_ref[0] = proj(C_HY_U, C_HY_G)
    hysg_ref[0] = _silu(proj(C_HY_G, C_GQ_Q)).astype(BF16)

    cos, sin = cos_ref[...], sin_ref[...]
    scale = HEAD_DIM ** -0.5
    qt = proj(C_GQ_Q, C_GQ_K).T.reshape(GQA_HEADS, HEAD_DIM, tm)
    qn = (_norm_rope_t(qt, qg_ref[...], cos, sin) * (scale * LOG2E)).astype(BF16)
    for g in range(GQA_KV_HEADS):
        for a in range(tm // tq):
            for hh in range(GQA_GROUP):
                c0 = (a * GQA_GROUP + hh) * tq
                gqq_ref[0, g, :, c0:c0 + tq] = qn[g * GQA_GROUP + hh][:, a * tq:(a + 1) * tq]
    kt = proj(C_GQ_K, C_GQ_V).T.reshape(GQA_KV_HEADS, HEAD_DIM, tm)
    kt = _norm_rope_t(kt, kg_ref[...], cos, sin)
    for g in range(GQA_KV_HEADS):
        gqk_ref[0, g] = kt[g].T.astype(BF16)
    ones_rows = (lax.broadcasted_iota(jnp.int32, (ONES_ROWS, tm), 0) == 0).astype(BF16)
    vt = proj(C_GQ_V, C_GQ_G).T.astype(BF16)
    for g in range(GQA_KV_HEADS):
        gqv_ref[0, g, :HEAD_DIM, :] = vt[g * HEAD_DIM:(g + 1) * HEAD_DIM]
        gqv_ref[0, g, HEAD_DIM:, :] = ones_rows
    gqsg_ref[0] = _silu(proj(C_GQ_G, C_DF_Q)).astype(BF16)

    dfq_ref[0] = (proj(C_DF_Q, C_DF_K) * (scale * LOG2E)).T.reshape(DIFF_HEADS, 2 * HEAD_DIM, tm).astype(BF16)
    dk = proj(C_DF_K, C_DF_V).astype(BF16)
    for j in range(2 * DIFF_HEADS):
        dfk_ref[0, j] = dk[:, j * HEAD_DIM:(j + 1) * HEAD_DIM]
    dvt = proj(C_DF_V, C_DF_G).T.astype(BF16)
    for hh in range(DIFF_HEADS):
        dfv_ref[0, hh, :2 * HEAD_DIM, :] = dvt[hh * 2 * HEAD_DIM:(hh + 1) * 2 * HEAD_DIM]
        dfv_ref[0, hh, 2 * HEAD_DIM:, :] = ones_rows
    dfsg_ref[0] = _silu(proj(C_DF_G, IN_COLS)).astype(BF16)


def _in_proj(x, norm_g, w_in_bf, q_g, k_g, cos_t, sin_t, tm, tq):
    B, L, _ = x.shape
    assert tm % tq == 0
    nt = L // tm
    qg_tab = jnp.broadcast_to(q_g[:, None], (HEAD_DIM, tm))
    kg_tab = jnp.broadcast_to(k_g[:, None], (HEAD_DIM, tm))
    full = lambda shape: pl.BlockSpec(shape, lambda b, i: (0,) * len(shape))
    out_shapes = (
        jax.ShapeDtypeStruct((B, L, 3 * HY_WIDTH), F32),
        jax.ShapeDtypeStruct((B, L, HY_WIDTH), BF16),
        jax.ShapeDtypeStruct((B, GQA_KV_HEADS, HEAD_DIM, GQA_GROUP * L), BF16),
        jax.ShapeDtypeStruct((B, GQA_KV_HEADS, L, HEAD_DIM), BF16),
        jax.ShapeDtypeStruct((B, GQA_KV_HEADS, HEAD_DIM + ONES_ROWS, L), BF16),
        jax.ShapeDtypeStruct((B, L, GQA_HEADS * HEAD_DIM), BF16),
        jax.ShapeDtypeStruct((B, DIFF_HEADS, 2 * HEAD_DIM, L), BF16),
        jax.ShapeDtypeStruct((B, 2 * DIFF_HEADS, L, HEAD_DIM), BF16),
        jax.ShapeDtypeStruct((B, DIFF_HEADS, 2 * HEAD_DIM + ONES_ROWS, L), BF16),
        jax.ShapeDtypeStruct((B, L, DIFF_HEADS * 2 * HEAD_DIM), BF16),
    )
    tok = lambda w: pl.BlockSpec((1, tm, w), lambda b, i: (b, i, 0))
    tr = lambda h, d: pl.BlockSpec((1, h, d, tm), lambda b, i: (b, 0, 0, i))
    rows = lambda h: pl.BlockSpec((1, h, tm, HEAD_DIM), lambda b, i: (b, 0, i, 0))
    return pl.pallas_call(
        functools.partial(_in_proj_kernel, tq),
        grid=(B, nt),
        in_specs=[
            tok(D_MODEL),
            full((1, D_MODEL)),
            full((D_MODEL, IN_COLS)),
            full((HEAD_DIM, tm)),
            full((HEAD_DIM, tm)),
            pl.BlockSpec((HEAD_DIM // 2, tm), lambda b, i: (0, i)),
            pl.BlockSpec((HEAD_DIM // 2, tm), lambda b, i: (0, i)),
        ],
        out_specs=(
            tok(3 * HY_WIDTH), tok(HY_WIDTH),
            pl.BlockSpec((1, GQA_KV_HEADS, HEAD_DIM, GQA_GROUP * tm), lambda b, i: (b, 0, 0, i)),
            rows(GQA_KV_HEADS), tr(GQA_KV_HEADS, HEAD_DIM + ONES_ROWS),
            tok(GQA_HEADS * HEAD_DIM),
            tr(DIFF_HEADS, 2 * HEAD_DIM), rows(2 * DIFF_HEADS), tr(DIFF_HEADS, 2 * HEAD_DIM + ONES_ROWS),
            tok(DIFF_HEADS * 2 * HEAD_DIM),
        ),
        out_shape=out_shapes,
        compiler_params=_cparams(("parallel", "parallel")),
        name="in_proj",
    )(x, norm_g.reshape(1, D_MODEL), w_in_bf, qg_tab, kg_tab, cos_t, sin_t)


def _hy_pre_kernel(u_ref, prev_ref, next_ref, w_ref, b_ref, sg_ref, hv_ref, x0g_ref):
    i = pl.program_id(1)
    nt = pl.num_programs(1)
    u = u_ref[0]
    tl = u.shape[0]
    prev_row = jnp.where(i > 0, prev_ref[0, 0, 7:8, :], 0.0)
    next_row = jnp.where(i < nt - 1, next_ref[0, 0, 0:1, :], 0.0)
    row = lax.broadcasted_iota(jnp.int32, u.shape, 0)
    up = jnp.where(row == 0, prev_row, pltpu.roll(u, 1, 0))
    dn = jnp.where(row == tl - 1, next_row, pltpu.roll(u, tl - 1, 0))
    w = w_ref[...]
    hy = up * w[0:1] + u * w[1:2] + dn * w[2:3] + b_ref[...]
    x0 = hy[:, :HY_WIDTH]
    x1 = hy[:, HY_WIDTH:2 * HY_WIDTH]
    hv = hy[:, 2 * HY_WIDTH:]
    hv_ref[0] = hv * x1
    x0g_ref[0] = (x0 * sg_ref[0].astype(F32)).astype(x0g_ref.dtype)


def _hy_pre(hy_u, conv_w, conv_b, hy_sg, tl):
    B, L, W3 = hy_u.shape
    nt = L // tl
    g8 = tl // 8
    u4 = hy_u.reshape(B, L // 8, 8, W3)
    return pl.pallas_call(
        _hy_pre_kernel,
        grid=(B, nt),
        in_specs=[
            pl.BlockSpec((1, tl, W3), lambda b, i: (b, i, 0)),
            pl.BlockSpec((1, 1, 8, W3), lambda b, i: (b, jnp.maximum(i * g8 - 1, 0), 0, 0)),
            pl.BlockSpec((1, 1, 8, W3), lambda b, i: (b, jnp.minimum((i + 1) * g8, L // 8 - 1), 0, 0)),
            pl.BlockSpec((3, W3), lambda b, i: (0, 0)),
            pl.BlockSpec((1, W3), lambda b, i: (0, 0)),
            pl.BlockSpec((1, tl, HY_WIDTH), lambda b, i: (b, i, 0)),
        ],
        out_specs=(
            pl.BlockSpec((1, tl, HY_WIDTH), lambda b, i: (b, i, 0)),
            pl.BlockSpec((1, tl, HY_WIDTH), lambda b, i: (b, i, 0)),
        ),
        out_shape=(
            jax.ShapeDtypeStruct((B, L, HY_WIDTH), F32),
            jax.ShapeDtypeStruct((B, L, HY_WIDTH), BF16),
        ),
        compiler_params=_cparams(("parallel", "parallel")),
        name="hy_pre",
    )(hy_u, u4, u4, conv_w, conv_b.reshape(1, W3), hy_sg)


def _hy_filter_kernel(z_ref, win_ref, w1_ref, b1_ref, w2_ref, b2_ref, wout_ref, freq_ref, kf_ref, kb_ref):
    i = pl.program_id(0)
    freq = freq_ref[...]
    a = jnp.sin(freq * (jnp.dot(z_ref[...], w1_ref[...], precision=HI, preferred_element_type=F32) + b1_ref[...]))
    for j in range(FILT_INNER):
        a = jnp.sin(freq * (jnp.dot(a, w2_ref[j], precision=HI, preferred_element_type=F32) + b2_ref[j:j + 1, :]))
    hf = jnp.dot(a, wout_ref[...], precision=HI, preferred_element_type=F32)
    win = win_ref[...]
    kf_ref[...] = hf[:, :HY_WIDTH] * win
    row = lax.broadcasted_iota(jnp.int32, win.shape, 0)
    kb_ref[...] = jnp.where((row == 0) & (i == 0), 0.0, hf[:, HY_WIDTH:] * win)


def _hy_filter(z_pad, window, w1_pad, b1, w2, b2, wout, freq, tl):
    L = z_pad.shape[0]
    full = lambda shape: pl.BlockSpec(shape, lambda i: (0,) * len(shape))
    return pl.pallas_call(
        _hy_filter_kernel,
        grid=(L // tl,),
        in_specs=[
            pl.BlockSpec((tl, FILT_EMB_PAD), lambda i: (i, 0)),
            pl.BlockSpec((tl, HY_WIDTH), lambda i: (i, 0)),
            full((FILT_EMB_PAD, FILT_ORDER)),
            full((1, FILT_ORDER)),
            full((FILT_INNER, FILT_ORDER, FILT_ORDER)),
            full((FILT_INNER, FILT_ORDER)),
            full((FILT_ORDER, 2 * HY_WIDTH)),
            full((1, FILT_ORDER)),
        ],
        out_specs=(
            pl.BlockSpec((tl, HY_WIDTH), lambda i: (i, 0)),
            pl.BlockSpec((tl, HY_WIDTH), lambda i: (i, 0)),
        ),
        out_shape=(
            jax.ShapeDtypeStruct((L, HY_WIDTH), F32),
            jax.ShapeDtypeStruct((L, HY_WIDTH), F32),
        ),
        compiler_params=_cparams(("parallel",)),
        name="hy_filter",
    )(z_pad, window, w1_pad, b1.reshape(1, -1), w2, b2, wout, freq.reshape(1, -1))


class _FftPlan:
    def __init__(self, L):
        n = 2 * L
        e = int(round(math.log2(n)))
        assert 2 ** e == n and e >= 8
        self.L, self.n = L, n
        self.n1 = 2 ** (e // 2)
        self.n2 = n // self.n1
        self.h1 = self.n1 // 2
        self.kh = self.h1 + 8
        self.kv = self.h1 + 1
        n1, n2, h1, kh = self.n1, self.n2, self.h1, self.kh
        k1 = np.arange(kh)[:, None]
        a = 2 * np.pi * k1 * np.arange(h1)[None, :] / n1
        self.fa = np.concatenate([np.cos(a), -np.sin(a)], axis=0).astype(np.float32)
        a = 2 * np.pi * np.arange(n2)[:, None] * np.arange(n2)[None, :] / n2
        self.f2r, self.f2i = np.cos(a).astype(np.float32), (-np.sin(a)).astype(np.float32)
        a = 2 * np.pi * k1 * np.arange(n2)[None, :] / n
        self.twr, self.twi = np.cos(a).astype(np.float32), (-np.sin(a)).astype(np.float32)

    def fwd_tables(self):
        return [jnp.asarray(t) for t in (self.fa, self.f2r, self.f2i, self.twr, self.twi)]

    def conv_tables(self):
        n, n1, n2, h1, kh = self.n, self.n1, self.n2, self.h1, self.kh
        k1 = np.arange(kh)[None, :, None]
        a = 2 * np.pi * k1 * (np.arange(h1)[None, None, :] / n1 + np.arange(n2)[:, None, None] / n)
        fa_tw = np.concatenate([np.cos(a), -np.sin(a)], axis=1)
        f2r, f2i = self.f2r.astype(np.float64), self.f2i.astype(np.float64)
        fwd = np.block([[f2r, -f2i], [f2i, f2r]])
        inv = np.block([[f2r, f2i], [-f2i, f2r]])
        wgt = np.where(np.arange(kh) <= h1, 2.0, 0.0)
        wgt[0] = 1.0
        wgt[h1] = 1.0
        k1 = np.arange(kh)[None, None, :]
        a = 2 * np.pi * k1 * (np.arange(h1)[None, :, None] / n1 + np.arange(n2)[:, None, None] / n)
        g = np.concatenate([np.cos(a) * wgt / n, -np.sin(a) * wgt / n], axis=2)
        return [jnp.asarray(t, dtype=BF16) for t in (fa_tw, fwd, inv, g)]


def _dot_exact(a, b):
    ah, bh = a.astype(BF16), b.astype(BF16)
    al, bl = (a - ah.astype(F32)).astype(BF16), (b - bh.astype(F32)).astype(BF16)
    return _dotf(ah, bh) + (_dotf(ah, bl) + _dotf(al, bh))


def _dft_cols(plan, x_ref, fa_ref, pr_ref, pi_ref, dot):
    fa = fa_ref[...]

    def body(n2, _):
        x = x_ref[pl.ds(n2, plan.h1, stride=plan.n2), :]
        y = dot(fa, x)
        pr_ref[pl.ds(n2, plan.kh, stride=plan.n2), :] = y[:plan.kh]
        pi_ref[pl.ds(n2, plan.kh, stride=plan.n2), :] = y[plan.kh:]
        return 0

    lax.fori_loop(0, plan.n2, body, 0, unroll=4)


def _twiddled_row_dft(plan, k1, f2r, f2i, twr_ref, twi_ref, ar, ai, dot):
    twr = twr_ref[pl.ds(k1, 1), :]
    twi = twi_ref[pl.ds(k1, 1), :]
    mr = f2r * twr - f2i * twi
    mi = f2r * twi + f2i * twr
    m = jnp.concatenate([jnp.concatenate([mr, -mi], axis=1), jnp.concatenate([mi, mr], axis=1)], axis=0)
    y = dot(m, jnp.concatenate([ar, ai], axis=0))
    return y[:plan.n2], y[plan.n2:]


def _hy_spectrum_kernel(plan, kf_ref, kb_ref, fa_ref, f2r_ref, f2i_ref, twr_ref, twi_ref, c_ref, pr_ref, pi_ref):
    f2r, f2i = f2r_ref[...], f2i_ref[...]
    n2 = plan.n2
    for which, src in enumerate((kf_ref, kb_ref)):
        _dft_cols(plan, src, fa_ref, pr_ref, pi_ref, _dot_exact)

        def body(k1, _):
            r0 = pl.multiple_of(k1 * n2, n2)
            xr, xi = _twiddled_row_dft(plan, k1, f2r, f2i, twr_ref, twi_ref,
                                       pr_ref[pl.ds(r0, n2), :], pi_ref[pl.ds(r0, n2), :], _dot_exact)
            if which == 0:
                c_ref[0, pl.ds(r0, n2), :] = xr
                c_ref[1, pl.ds(r0, n2), :] = xi
            else:
                c_ref[0, pl.ds(r0, n2), :] = c_ref[0, pl.ds(r0, n2), :] + xr
                c_ref[1, pl.ds(r0, n2), :] = c_ref[1, pl.ds(r0, n2), :] - xi
            return 0

        lax.fori_loop(0, plan.kh, body, 0, unroll=2)


def _hy_spectrum(plan, kf, kb):
    rows = plan.kh * plan.n2
    nc = HY_WIDTH // LANES
    tabs = plan.fwd_tables()
    full = lambda a: pl.BlockSpec(a.shape, lambda c: (0,) * a.ndim)
    return pl.pallas_call(
        functools.partial(_hy_spectrum_kernel, plan),
        grid=(nc,),
        in_specs=[pl.BlockSpec((plan.L, LANES), lambda c: (0, c)),
                  pl.BlockSpec((plan.L, LANES), lambda c: (0, c))] + [full(t) for t in tabs],
        out_specs=pl.BlockSpec((2, rows, LANES), lambda c: (0, 0, c)),
        out_shape=jax.ShapeDtypeStruct((2, rows, HY_WIDTH), F32),
        scratch_shapes=[pltpu.VMEM((rows, LANES), F32), pltpu.VMEM((rows, LANES), F32)],
        compiler_params=_cparams(("parallel",)),
        name="hy_spectrum",
    )(kf, kb, *tabs)


SUBLANES = 8


def _hy_conv_kernel(plan, hv_ref, c_ref, x0g_ref, bias_ref, fa_ref, fwd_ref, inv_ref, g_ref, o_ref,
                    pr_ref, pi_ref, qr_ref, qi_ref, y_ref):
    n2, kh, h1 = plan.n2, plan.kh, plan.h1
    eight = lambda a: pl.ds(pl.multiple_of(a * SUBLANES, SUBLANES), SUBLANES)

    def cols_body(a, _):
        slab = jnp.swapaxes(hv_ref[0, :, eight(a), :], 0, 1).astype(BF16)
        for s in range(SUBLANES):
            j = a * SUBLANES + s
            y = _dotf(fa_ref[j], slab[s])
            pr_ref[j] = y[:kh]
            pi_ref[j] = y[kh:]
        return 0

    lax.fori_loop(0, n2 // SUBLANES, cols_body, 0, unroll=2)

    fwd, inv = fwd_ref[...], inv_ref[...]

    def rows_body(c, _):
        xr8 = jnp.swapaxes(pr_ref[:, eight(c), :], 0, 1)
        xi8 = jnp.swapaxes(pi_ref[:, eight(c), :], 0, 1)
        for d in range(0, SUBLANES, 2):
            k1 = c * SUBLANES + d
            x = jnp.concatenate([jnp.concatenate([xr8[d], xr8[d + 1]], axis=1),
                                 jnp.concatenate([xi8[d], xi8[d + 1]], axis=1)], axis=0)
            xf = _dotf(fwd, x.astype(BF16))
            xr, xi = xf[:n2], xf[n2:]
            cr = jnp.concatenate([c_ref[0, k1], c_ref[0, k1 + 1]], axis=1)
            ci = jnp.concatenate([c_ref[1, k1], c_ref[1, k1 + 1]], axis=1)
            z = jnp.concatenate([xr * cr - xi * ci, xr * ci + xi * cr], axis=0)
            b = _dotf(inv, z.astype(BF16))
            w = b.shape[1] // 2
            for e in range(2):
                qr_ref[k1 + e] = b[:n2, e * w:(e + 1) * w]
                qi_ref[k1 + e] = b[n2:, e * w:(e + 1) * w]
        return 0

    trips = kh // SUBLANES
    lax.fori_loop(0, trips, rows_body, 0, unroll=next(u for u in (3, 5, 2, 1) if trips % u == 0))

    def icols_body(a, _):
        br8 = jnp.swapaxes(qr_ref[:, eight(a), :], 0, 1)
        bi8 = jnp.swapaxes(qi_ref[:, eight(a), :], 0, 1)
        ys = [_dotf(g_ref[a * SUBLANES + s], jnp.concatenate([br8[s], bi8[s]], axis=0).astype(BF16))
              for s in range(SUBLANES)]
        y_ref[:, eight(a), :] = jnp.swapaxes(jnp.stack(ys, axis=0), 0, 1)
        return 0

    lax.fori_loop(0, n2 // SUBLANES, icols_body, 0, unroll=2)

    bias = bias_ref[...]
    rows = max(1, 512 // n2)

    def out_body(t, _):
        r = pl.ds(t * rows, rows)
        y = y_ref[r] + hv_ref[0, r] * bias
        o_ref[0, r] = (y * x0g_ref[0, r].astype(F32)).astype(o_ref.dtype)
        return 0

    lax.fori_loop(0, h1 // rows, out_body, 0)


def _hy_conv(plan, hv, spec, x0g, hy_bias):
    B, L, _ = hv.shape
    n2, kh, h1 = plan.n2, plan.kh, plan.h1
    cb = LANES
    nc = HY_WIDTH // cb
    tabs = plan.conv_tables()
    full = lambda a: pl.BlockSpec(a.shape, lambda c, b: (0,) * a.ndim)
    view = lambda a: a.reshape(B, h1, n2, HY_WIDTH)
    seq = pl.BlockSpec((1, h1, n2, cb), lambda c, b: (b, 0, 0, c))
    seq_in = pl.BlockSpec((1, h1, n2, cb), lambda c, b: (b, 0, 0, c), pipeline_mode=pl.Buffered(1))
    out = pl.pallas_call(
        functools.partial(_hy_conv_kernel, plan),
        grid=(nc, B),
        in_specs=[seq_in,
                  pl.BlockSpec((2, kh, n2, cb), lambda c, b: (0, 0, 0, c), pipeline_mode=pl.Buffered(1)),
                  seq_in,
                  pl.BlockSpec((1, cb), lambda c, b: (0, c))] + [full(t) for t in tabs],
        out_specs=seq,
        out_shape=jax.ShapeDtypeStruct((B, h1, n2, HY_WIDTH), BF16),
        scratch_shapes=[pltpu.VMEM((n2, kh, cb), F32), pltpu.VMEM((n2, kh, cb), F32),
                        pltpu.VMEM((kh, n2, cb), F32), pltpu.VMEM((kh, n2, cb), F32),
                        pltpu.VMEM((h1, n2, cb), F32)],
        compiler_params=_cparams(("parallel", "parallel")),
        name="hy_conv",
    )(view(hv), spec.reshape(2, kh, n2, HY_WIDTH), view(x0g), hy_bias.reshape(1, HY_WIDTH), *tabs)
    return out.reshape(B, L, HY_WIDTH)


def _blk(index, size):
    if isinstance(index, int):
        return pl.ds(index * size, size)
    return pl.ds(pl.multiple_of(index * size, size), size)


def _chunks_per_trip(n):
    for u in (8, 4, 2):
        if n % u == 0:
            return u
    raise ValueError(f"need an even number of key chunks, got {n}")


LAG_JUMP_LIMIT = 32.0


def _flash_pipeline(n, score_fn, v_chunk_fn, m_ref, acc_ref, s_refs, mp_refs):
    u = _chunks_per_trip(n)
    m_ref[...] = jnp.full(m_ref.shape, -jnp.inf, F32)
    acc_ref[...] = jnp.zeros(acc_ref.shape, F32)

    def scores(j, slot):
        s = score_fn(j)
        s_refs[slot][...] = s
        mp_refs[slot][...] = jnp.max(s, axis=0, keepdims=True)

    def update(j, slot):
        m_old = m_ref[...]
        m_new = jnp.maximum(m_old, mp_refs[slot][...])
        alpha = jnp.exp2(m_old - m_new)
        p = jnp.exp2(s_refs[slot][...] - m_new).astype(BF16)
        acc_ref[...] = alpha * acc_ref[...] + _dotf(v_chunk_fn(j), p)
        m_ref[...] = m_new

    scores(0, 0)

    def body(jj, _):
        j = u * jj
        for d in range(u):
            scores(j + d + 1 if d + 1 < u else jnp.minimum(j + u, n - 1), (d + 1) % 2)
            update(j + d, d % 2)
        return 0

    lax.fori_loop(0, n // u, body, 0)


def _flash_lagged_tiles(nt, n, score_fn, v_chunk_fn, finish_fn, m_ref, acc_ref, done_ref, jump_ref, s_ref):
    s0 = score_fn(0, 0)
    s_ref[...] = s0
    m_ref[...] = jnp.max(s0, axis=0, keepdims=True)
    acc_ref[...] = jnp.zeros(acc_ref.shape, F32)
    done_ref[...] = jnp.ones(done_ref.shape, F32)
    jump_ref[...] = jnp.zeros(jump_ref.shape, F32)

    def step(j, s):
        m_used = m_ref[...]
        p = jnp.exp2(s - m_used).astype(BF16)
        rise = jnp.log2(jnp.max(p, axis=0, keepdims=True).astype(F32))
        up = jnp.maximum(rise, 0.0)
        acc_ref[...] = (acc_ref[...] + _dotf(v_chunk_fn(j), p)) * jnp.exp2(-up)
        jump_ref[...] = jnp.maximum(jump_ref[...], rise)
        m_ref[...] = m_used + up

    def body(i, _):
        s_cur = s_ref[...]
        for d in range(n):
            s_next = score_fn(i, d + 1) if d + 1 < n else score_fn(jnp.minimum(i + 1, nt - 1), 0)
            step(d, s_cur)
            if d == 0:
                finish_fn(jnp.maximum(i - 1, 0), done_ref[...])
            s_cur = s_next
        s_ref[...] = s_cur
        done_ref[...] = acc_ref[...]
        acc_ref[...] = jnp.zeros(acc_ref.shape, F32)
        m_ref[...] = jnp.max(s_cur, axis=0, keepdims=True)
        return 0

    lax.fori_loop(0, nt, body, 0)
    finish_fn(nt - 1, done_ref[...])


def _lagged_scratch(rows, tk, nq):
    acc = pltpu.VMEM((rows + ONES_ROWS, nq), F32)
    return [pltpu.VMEM((1, nq), F32), acc, acc, pltpu.VMEM((tk, nq), F32)]


def _exact_scratch(rows, tk, nq):
    return [pltpu.VMEM((1, nq), F32), pltpu.VMEM((rows + ONES_ROWS, nq), F32),
            pltpu.VMEM((tk, nq), F32), pltpu.VMEM((tk, nq), F32),
            pltpu.VMEM((1, nq), F32), pltpu.VMEM((1, nq), F32)]


def _with_exact_fallback(run):
    *outs, jump = run(True)
    outs = tuple(outs)
    return lax.cond(jnp.max(jump) > LAG_JUMP_LIMIT, lambda: tuple(run(False)), lambda: outs)


def _gqa_finish(acc, sg, tq):
    o = acc[:HEAD_DIM] * (1.0 / acc[HEAD_DIM:HEAD_DIM + 1])
    ot = jnp.concatenate([o[:, h * tq:(h + 1) * tq].T for h in range(GQA_GROUP)], axis=1)
    return (ot * sg.astype(F32)).astype(BF16)


def _gqa_lagged_kernel(tq, tk, q_ref, k_ref, v_ref, sg_ref, o_ref, jump_ref, m_ref, acc_ref, done_ref, s_ref):
    L = k_ref.shape[2]
    nq = GQA_GROUP * tq

    def score_fn(i, j):
        qt = q_ref[0, 0, :, _blk(i, nq)]
        return _dotf(k_ref[0, 0, _blk(j, tk), :], qt)

    def v_chunk_fn(j):
        return v_ref[0, 0, :, _blk(j, tk)]

    def finish_fn(i, acc):
        rows = _blk(i, tq)
        o_ref[0, rows, :] = _gqa_finish(acc, sg_ref[0, rows, :], tq)

    _flash_lagged_tiles(L // tq, L // tk, score_fn, v_chunk_fn, finish_fn, m_ref, acc_ref, done_ref,
                        jump_ref.at[0], s_ref)


def _gqa_exact_kernel(tk, q_ref, k_ref, v_ref, sg_ref, o_ref, m_ref, acc_ref, s0_ref, s1_ref, mp0_ref, mp1_ref):
    qt = q_ref[0, 0]
    tq = qt.shape[-1] // GQA_GROUP
    L = k_ref.shape[2]

    def score_fn(j):
        return _dotf(k_ref[0, 0, pl.ds(pl.multiple_of(j * tk, tk), tk), :], qt)

    def v_chunk_fn(j):
        return v_ref[0, 0, :, pl.ds(pl.multiple_of(j * tk, tk), tk)]

    _flash_pipeline(L // tk, score_fn, v_chunk_fn, m_ref, acc_ref, (s0_ref, s1_ref), (mp0_ref, mp1_ref))
    o_ref[0] = _gqa_finish(acc_ref[...], sg_ref[0], tq)


def _gqa_attn(qT, k, vT, sg, tq, tk):
    B, _, _, L4 = qT.shape
    L = L4 // GQA_GROUP
    gw = GQA_GROUP * HEAD_DIM
    nq = GQA_GROUP * tq
    nt = L // tq
    y_shape = jax.ShapeDtypeStruct((B, L, GQA_HEADS * HEAD_DIM), BF16)

    def run(lagged):
        if lagged:
            whole = lambda shape: pl.BlockSpec((1, 1) + shape, lambda b, g: (b, g, 0, 0))
            y_spec = pl.BlockSpec((1, L, gw), lambda b, g: (b, 0, g))
            return pl.pallas_call(
                functools.partial(_gqa_lagged_kernel, tq, tk),
                grid=(B, GQA_KV_HEADS),
                in_specs=[whole((HEAD_DIM, L4)), whole((L, HEAD_DIM)), whole((HEAD_DIM + ONES_ROWS, L)), y_spec],
                out_specs=(y_spec, pl.BlockSpec((1, 1, nq), lambda b, g: (b * GQA_KV_HEADS + g, 0, 0))),
                out_shape=(y_shape, jax.ShapeDtypeStruct((B * GQA_KV_HEADS, 1, nq), F32)),
                scratch_shapes=_lagged_scratch(HEAD_DIM, tk, nq),
                compiler_params=_cparams(("parallel", "parallel")),
                name="gqa_attn",
            )(qT, k, vT, sg)
        y_spec = pl.BlockSpec((1, tq, gw), lambda b, g, i: (b, i, g))
        return (pl.pallas_call(
            functools.partial(_gqa_exact_kernel, tk),
            grid=(B, GQA_KV_HEADS, nt),
            in_specs=[pl.BlockSpec((1, 1, HEAD_DIM, nq), lambda b, g, i: (b, g, 0, i)),
                      pl.BlockSpec((1, 1, L, HEAD_DIM), lambda b, g, i: (b, g, 0, 0)),
                      pl.BlockSpec((1, 1, HEAD_DIM + ONES_ROWS, L), lambda b, g, i: (b, g, 0, 0)),
                      y_spec],
            out_specs=y_spec,
            out_shape=y_shape,
            scratch_shapes=_exact_scratch(HEAD_DIM, tk, nq),
            compiler_params=_cparams(("parallel", "parallel", "parallel")),
            name="gqa_attn_exact",
        )(qT, k, vT, sg),)

    return _with_exact_fallback(run)[0]


def _t5_bias_kernel(t, rb_ref, o_ref):
    h = pl.program_id(0)
    kk = lax.broadcasted_iota(jnp.int32, (t, t), 0)
    qq = lax.broadcasted_iota(jnp.int32, (t, t), 1)
    nb = REL_BUCKETS // 2
    max_exact = nb // 2
    thresholds = [int(math.ceil(max_exact * (REL_MAX_DIST / max_exact) ** (j / (nb - max_exact)) - 1e-9))
                  for j in range(1, nb - max_exact)]
    for idx in range(5):
        rel = (idx - 2) * t + kk - qq
        n = jnp.abs(rel)
        large = jnp.full((t, t), max_exact, jnp.int32)
        for th in thresholds:
            large = large + (n >= th).astype(jnp.int32)
        bucket = jnp.where(rel > 0, nb, 0) + jnp.where(n < max_exact, n, large)
        val = jnp.zeros((t, t), F32)
        for b in range(REL_BUCKETS):
            val = jnp.where(bucket == b, rb_ref[h, b], val)
        o_ref[0, idx] = val * LOG2E


def _t5_bias_tiles(rel_bias, t):
    return pl.pallas_call(
        functools.partial(_t5_bias_kernel, t),
        grid=(DIFF_HEADS,),
        in_specs=[pl.BlockSpec(memory_space=pltpu.SMEM)],
        out_specs=pl.BlockSpec((1, 5, t, t), lambda h: (h, 0, 0, 0)),
        out_shape=jax.ShapeDtypeStruct((DIFF_HEADS, 5, t, t), F32),
        compiler_params=_cparams(("parallel",)),
        name="t5_bias_tiles",
    )(rel_bias.T)


def _diff_scores(t, tk, i, j, qt, k0_ref, k1_ref, bias_ref):
    tiles = tk // t
    bias = jnp.concatenate([bias_ref[0, jnp.clip(j * tiles + r - i, -2, 2) + 2] for r in range(tiles)],
                           axis=0)
    return jnp.concatenate(
        [_dotf(k0_ref[0, 0, _blk(j, tk), :], qt[:HEAD_DIM]) + bias,
         _dotf(k1_ref[0, 0, _blk(j, tk), :], qt[HEAD_DIM:]) + bias], axis=1)


def _diff_finish(acc, t, lam_init, lam_ref, g_ref, sg):
    hw = 2 * HEAD_DIM
    lp = lam_ref[...]
    lam = (jnp.exp(jnp.sum(lp[0:1] * lp[1:2], axis=1, keepdims=True))
           - jnp.exp(jnp.sum(lp[2:3] * lp[3:4], axis=1, keepdims=True)) + lam_init)
    o = acc[:hw] * (1.0 / acc[hw:hw + 1])
    o = o[:, :t] - lam * o[:, t:]
    ms = jnp.mean(o * o, axis=0, keepdims=True)
    o = o * lax.rsqrt(ms + EPS) * g_ref[...] * (1.0 - lam_init)
    return (o.T * sg.astype(F32)).astype(BF16)


def _diff_lagged_kernel(t, tk, lam_init, q_ref, k0_ref, k1_ref, v_ref, bias_ref, lam_ref, g_ref, sg_ref,
                        o_ref, jump_ref, m_ref, acc_ref, done_ref, s_ref):
    L = k0_ref.shape[2]

    def score_fn(i, j):
        return _diff_scores(t, tk, i, j, q_ref[0, 0, :, _blk(i, t)], k0_ref, k1_ref, bias_ref)

    def v_chunk_fn(j):
        return v_ref[0, 0, :, _blk(j, tk)]

    def finish_fn(i, acc):
        rows = _blk(i, t)
        o_ref[0, rows, :] = _diff_finish(acc, t, lam_init, lam_ref, g_ref, sg_ref[0, rows, :])

    _flash_lagged_tiles(L // t, L // tk, score_fn, v_chunk_fn, finish_fn, m_ref, acc_ref, done_ref,
                        jump_ref.at[0], s_ref)


def _diff_exact_kernel(t, tk, lam_init, q_ref, k0_ref, k1_ref, v_ref, bias_ref, lam_ref, g_ref, sg_ref,
                       o_ref, m_ref, acc_ref, s0_ref, s1_ref, mp0_ref, mp1_ref):
    i = pl.program_id(2)
    L = k0_ref.shape[2]
    qt = q_ref[0, 0]

    def score_fn(j):
        return _diff_scores(t, tk, i, j, qt, k0_ref, k1_ref, bias_ref)

    def v_chunk_fn(j):
        return v_ref[0, 0, :, _blk(j, tk)]

    _flash_pipeline(L // tk, score_fn, v_chunk_fn, m_ref, acc_ref, (s0_ref, s1_ref), (mp0_ref, mp1_ref))
    o_ref[0] = _diff_finish(acc_ref[...], t, lam_init, lam_ref, g_ref, sg_ref[0])


def _diff_attn(qT, k, vT, bias_tiles, lam_params, subln_g, sg, lam_init, t, tk):
    B, _, _, L = qT.shape
    hw = 2 * HEAD_DIM
    nt = L // t
    g_tab = jnp.broadcast_to(subln_g[:, None], (hw, t))
    y_shape = jax.ShapeDtypeStruct((B, L, DIFF_HEADS * hw), BF16)

    def run(lagged):
        nd = 2 if lagged else 3

        def spec(block, index):
            return pl.BlockSpec(block, (lambda b, h: index(b, h, 0)) if lagged else index)

        tq_blk = L if lagged else t
        y_spec = spec((1, tq_blk, hw), lambda b, h, i: (b, i, h))
        in_specs = [spec((1, 1, hw, tq_blk), lambda b, h, i: (b, h, 0, i)),
                    spec((1, 1, L, HEAD_DIM), lambda b, h, i: (b, 2 * h, 0, 0)),
                    spec((1, 1, L, HEAD_DIM), lambda b, h, i: (b, 2 * h + 1, 0, 0)),
                    spec((1, 1, hw + ONES_ROWS, L), lambda b, h, i: (b, h, 0, 0)),
                    spec((1, 5, t, t), lambda b, h, i: (h, 0, 0, 0)),
                    spec((4, HEAD_DIM), lambda b, h, i: (0, 0)),
                    spec((hw, t), lambda b, h, i: (0, 0)),
                    y_spec]
        args = (qT, k, k, vT, bias_tiles, lam_params, g_tab, sg)
        if lagged:
            return pl.pallas_call(
                functools.partial(_diff_lagged_kernel, t, tk, lam_init),
                grid=(B, DIFF_HEADS),
                in_specs=in_specs,
                out_specs=(y_spec, pl.BlockSpec((1, 1, 2 * t), lambda b, h: (b * DIFF_HEADS + h, 0, 0))),
                out_shape=(y_shape, jax.ShapeDtypeStruct((B * DIFF_HEADS, 1, 2 * t), F32)),
                scratch_shapes=_lagged_scratch(hw, tk, 2 * t),
                compiler_params=_cparams(("parallel",) * nd),
                name="diff_attn",
            )(*args)
        return (pl.pallas_call(
            functools.partial(_diff_exact_kernel, t, tk, lam_init),
            grid=(B, DIFF_HEADS, nt),
            in_specs=in_specs,
            out_specs=y_spec,
            out_shape=y_shape,
            scratch_shapes=_exact_scratch(hw, tk, 2 * t),
            compiler_params=_cparams(("parallel",) * nd),
            name="diff_attn_exact",
        )(*args),)

    return _with_exact_fallback(run)[0]


def _merge_kernel(final, x_ref, ng_ref, yh_ref, yg_ref, yd_ref, wm_ref, bm_ref, wh_ref, wg_ref, wd_ref,
                  wo_ref, fg_ref, o_ref):
    x = x_ref[...]
    ms = jnp.mean(x * x, axis=-1, keepdims=True)
    h = (x * lax.rsqrt(ms + EPS) * ng_ref[...]).astype(BF16)
    merged = None
    for b, (y_ref, w_ref) in enumerate(((yh_ref, wh_ref), (yg_ref, wg_ref), (yd_ref, wd_ref))):
        z = _dotf(h, wm_ref[:, b * D_MODEL:(b + 1) * D_MODEL]) + bm_ref[:, b * D_MODEL:(b + 1) * D_MODEL]
        gate = 1.0 / (1.0 + jnp.exp(-z))
        term = gate * _dotf(y_ref[...], w_ref[...])
        merged = term if merged is None else merged + term
    y = x + _dotf(merged.astype(BF16), wo_ref[...])
    if final:
        ms = jnp.mean(y * y, axis=-1, keepdims=True)
        y = y * lax.rsqrt(ms + EPS) * fg_ref[...]
    o_ref[...] = y


def _merge_out(x2, norm_g, y_hy, y_gq, y_df, w_merge, b_merge, w_hy, w_gq, w_df, w_out, final_g, final, tm):
    T = x2.shape[0]
    full = lambda a: pl.BlockSpec(a.shape, lambda i: (0,) * a.ndim)
    tok = lambda w: pl.BlockSpec((tm, w), lambda i: (i, 0))
    ng = norm_g.reshape(1, D_MODEL)
    bm = b_merge.reshape(1, -1)
    fg = final_g.reshape(1, D_MODEL)
    return pl.pallas_call(
        functools.partial(_merge_kernel, final),
        grid=(T // tm,),
        in_specs=[tok(D_MODEL), full(ng), tok(HY_WIDTH), tok(GQA_HEADS * HEAD_DIM), tok(DIFF_HEADS * 2 * HEAD_DIM),
                  full(w_merge), full(bm), full(w_hy), full(w_gq), full(w_df), full(w_out), full(fg)],
        out_specs=tok(D_MODEL),
        out_shape=jax.ShapeDtypeStruct((T, D_MODEL), F32),
        compiler_params=_cparams(("parallel",)),
        name="merge_out",
    )(x2, ng, y_hy, y_gq, y_df, w_merge, bm, w_hy, w_gq, w_df, w_out, fg)


def _rope_tables_t(L):
    rows = L // GRID_W
    row = jnp.broadcast_to(jnp.arange(rows, dtype=F32)[:, None], (rows, GRID_W)).reshape(L)
    col = jnp.broadcast_to(jnp.arange(GRID_W, dtype=F32)[None, :], (rows, GRID_W)).reshape(L)
    n_freq = HEAD_DIM // 4
    inv_freq = ROPE_THETA ** (-jnp.arange(n_freq, dtype=F32) / n_freq)
    ang = jnp.concatenate([row[:, None] * inv_freq, col[:, None] * inv_freq], axis=-1)
    return jnp.cos(ang).T, jnp.sin(ang).T


def _hyena_positions(L):
    t01 = jnp.linspace(0.0, 1.0, L, dtype=F32)[:, None]
    bands = (FILT_EMB - 1) // 2
    w = 2.0 * math.pi * jnp.arange(L, dtype=F32)[:, None] / L
    f = jnp.linspace(1e-4, bands - 1, bands, dtype=F32)[None, :]
    z = jnp.concatenate([t01, jnp.cos(f * w), -jnp.sin(f * w)], axis=-1)
    z = jnp.pad(z, ((0, 0), (0, FILT_EMB_PAD - FILT_EMB)))
    max_decay = math.log(HY_TARGET) / HY_FAST_DECAY
    min_decay = math.log(HY_TARGET) / HY_SLOW_DECAY
    deltas = jnp.linspace(min_decay, max_decay, HY_WIDTH, dtype=F32)
    window = jnp.exp(-t01 * jnp.abs(deltas)[None, :]) + HY_SHIFT
    return z, window


def _tile(L, want):
    t = min(L, want)
    assert L % t == 0
    return t


def _trunk(x, p, bias_tiles, attn_t):
    B, L, _ = x.shape
    depth = p['w_in'].shape[0]
    cos_t, sin_t = _rope_tables_t(L)
    z_pad, window = _hyena_positions(L)
    plan = _FftPlan(L)
    tm = _tile(L, 512)
    gqa_tq = _tile(L, 128)
    x2 = x.reshape(B * L, D_MODEL)
    for l in range(depth):
        (hy_u, hy_sg, gq_qT, gq_k, gq_vT, gq_sg, df_qT, df_k, df_vT, df_sg) = _in_proj(
            x2.reshape(B, L, D_MODEL), p['norm_g'][l], p['w_in_bf'][l], p['q_norm_g'][l], p['k_norm_g'][l],
            cos_t, sin_t, tm, gqa_tq)

        hv, x0g = _hy_pre(hy_u, p['hy_conv_w'][l], p['hy_conv_b'][l], hy_sg, _tile(L, 512))
        w1_pad = jnp.pad(p['hy_f_w1'][l], ((0, FILT_EMB_PAD - FILT_EMB), (0, 0)))
        kf, kb = _hy_filter(z_pad, window, w1_pad, p['hy_f_b1'][l], p['hy_f_w2'][l], p['hy_f_b2'][l],
                            p['hy_f_wout'][l], p['hy_f_freq'][l], _tile(L, 512))
        spec = _hy_spectrum(plan, kf, kb)
        y_hy = _hy_conv(plan, hv, spec, x0g, p['hy_bias'][l])

        y_gq = _gqa_attn(gq_qT, gq_k, gq_vT, gq_sg, gqa_tq, _tile(L // 2, 256))

        lam_init = 0.8 - 0.6 * math.exp(-0.3 * l)
        lam_params = jnp.stack([p['lam_q1'][l], p['lam_k1'][l], p['lam_q2'][l], p['lam_k2'][l]])
        y_df = _diff_attn(df_qT, df_k, df_vT, bias_tiles, lam_params, p['diff_subln_g'][l], df_sg, lam_init,
                          attn_t, max(attn_t, _tile(L // 2, 512)))

        x2 = _merge_out(x2, p['norm_g'][l], y_hy.reshape(B * L, -1), y_gq.reshape(B * L, -1),
                        y_df.reshape(B * L, -1), p['w_merge_bf'][l], p['b_merge'][l], p['w_branch_hy_bf'][l],
                        p['w_branch_gqa_bf'][l], p['w_branch_diff_bf'][l], p['w_out_bf'][l], p['final_g'],
                        l == depth - 1, _tile(B * L, 512))
    return x2.reshape(B, L, D_MODEL)


def kernel(x_prompt, x_sample, rel_bias, norm_g, w_in, hy_conv_w, hy_conv_b, hy_f_w1, hy_f_b1, hy_f_w2, hy_f_b2, hy_f_wout, hy_f_freq, hy_bias, q_norm_g, k_norm_g, lam_q1, lam_k1, lam_q2, lam_k2, diff_subln_g, w_branch_hy, w_branch_gqa, w_branch_diff, w_merge, b_merge, w_out, final_g):
    p = dict(norm_g=norm_g, hy_conv_w=hy_conv_w, hy_conv_b=hy_conv_b, hy_f_w1=hy_f_w1,
             hy_f_b1=hy_f_b1, hy_f_w2=hy_f_w2, hy_f_b2=hy_f_b2, hy_f_wout=hy_f_wout, hy_f_freq=hy_f_freq,
             hy_bias=hy_bias, q_norm_g=q_norm_g, k_norm_g=k_norm_g, lam_q1=lam_q1, lam_k1=lam_k1,
             lam_q2=lam_q2, lam_k2=lam_k2, diff_subln_g=diff_subln_g, b_merge=b_merge, final_g=final_g,
             w_in=w_in,
             w_in_bf=w_in.astype(BF16), w_merge_bf=w_merge.astype(BF16), w_out_bf=w_out.astype(BF16),
             w_branch_hy_bf=w_branch_hy.astype(BF16), w_branch_gqa_bf=w_branch_gqa.astype(BF16),
             w_branch_diff_bf=w_branch_diff.astype(BF16))
    outs = []
    for x in (x_prompt, x_sample):
        attn_t = _tile(x.shape[1], 256)
        bias_tiles = _t5_bias_tiles(rel_bias, attn_t)
        outs.append(_trunk(x, p, bias_tiles, attn_t))
    return tuple(outs)
```
